```python
import math
import jax, jax.numpy as jnp
from jax import lax
import numpy as np

D_MODEL = 1024
BATCH = 2
SEQ = 8192
DEPTH = 4

N_A = DEPTH // 2
N_B = DEPTH - N_A
N_HEADS = 16
HEAD_DIM = D_MODEL // N_HEADS
D_FF = 4 * D_MODEL
CONV_W = 3
Q_BLOCK = 128
N_MOD = 6
EPS = 1e-6

kernel_name = "yoco_shortconv_fox_adaln_trunk"


def rms_norm(x, g):
    xf = x.astype(jnp.float32)
    y = xf * lax.rsqrt(jnp.mean(xf * xf, axis=-1, keepdims=True) + EPS)
    return y.astype(x.dtype) * g


def modulate(h, shift, scale):
    return h * (1 + scale[:, None, :]) + shift[:, None, :]


def short_conv_mixer(h, w_in, conv_w, w_out):
    d = h.shape[-1]
    u = h @ w_in
    b_g, c_g, xv = jnp.split(u, 3, axis=-1)
    z = c_g * xv
    zc = lax.conv_general_dilated(
        z, conv_w[:, None, :],
        window_strides=(1,), padding=[(CONV_W - 1, 0)],
        dimension_numbers=("NWC", "WIO", "NWC"),
        feature_group_count=d)
    return (b_g * zc) @ w_out


def squared_relu_mlp(h, w1, w2):
    a = jax.nn.relu(h @ w1)
    return (a * a) @ w2


def shared_kv(x, kv_norm_g, w_kv, k_norm_g, w_f, b_f):
    b, s, d = x.shape
    h = rms_norm(x, kv_norm_g)
    k, v = jnp.split(h @ w_kv, 2, axis=-1)
    k = rms_norm(k.reshape(b, s, N_HEADS, HEAD_DIM), k_norm_g)
    v = v.reshape(b, s, N_HEADS, HEAD_DIM)
    log_f = jax.nn.log_sigmoid((h @ w_f + b_f).astype(jnp.float32))
    F = jnp.cumsum(log_f, axis=1)
    return k, v, jnp.transpose(F, (0, 2, 1))


def fox_mixer(h, q_w, q_norm_g, o_w, k, v, F):
    b, s, d = h.shape
    nblk = s // Q_BLOCK
    q = rms_norm((h @ q_w).reshape(b, s, N_HEADS, HEAD_DIM), q_norm_g)
    q_blocks = q.reshape(b, nblk, Q_BLOCK, N_HEADS, HEAD_DIM).transpose(1, 0, 2, 3, 4)
    F_blocks = F.reshape(b, N_HEADS, nblk, Q_BLOCK).transpose(2, 0, 1, 3)
    offsets = jnp.arange(nblk, dtype=jnp.int32) * Q_BLOCK
    key_pos = jnp.arange(s, dtype=jnp.int32)
    scale = 1.0 / math.sqrt(HEAD_DIM)

    def one_block(args):
        qb, Fq, off = args
        logits = jnp.einsum("bqhd,bkhd->bhqk", qb, k).astype(jnp.float32) * scale
        logits = logits + (Fq[..., :, None] - F[..., None, :])
        q_pos = off + jnp.arange(Q_BLOCK, dtype=jnp.int32)
        causal = key_pos[None, :] <= q_pos[:, None]
        logits = jnp.where(causal[None, None], logits, -1e30)
        p = jax.nn.softmax(logits, axis=-1)
        return jnp.einsum("bhqk,bkhd->bqhd", p.astype(v.dtype), v)

    out = lax.map(one_block, (q_blocks, F_blocks, offsets))
    out = out.transpose(1, 0, 2, 3, 4).reshape(b, s, d)
    return out @ o_w


def setup_inputs(seed: int = 0) -> dict:
    key = jax.random.key(seed)
    ks = jax.random.split(key, 24)
    D, H, hd, FF = D_MODEL, N_HEADS, HEAD_DIM, D_FF
    nrm = lambda k, shape, s: jax.random.normal(k, shape, jnp.float32) * s
    gain = lambda k, shape: 1.0 + 0.05 * jax.random.normal(k, shape, jnp.float32)
    return {
        "x": nrm(ks[0], (BATCH, SEQ, D), 1.0),
        "c": nrm(ks[1], (BATCH, D), 1.0),
        "ada_w": nrm(ks[2], (DEPTH, D, N_MOD * D), 0.5 * D ** -0.5),
        "ada_b": nrm(ks[3], (DEPTH, N_MOD * D), 0.1),
        "norm_mix_g": gain(ks[4], (DEPTH, D)),
        "norm_mlp_g": gain(ks[5], (DEPTH, D)),
        "sc_w_in": nrm(ks[6], (N_A, D, 3 * D), D ** -0.5),
        "sc_conv": nrm(ks[7], (N_A, CONV_W, D), CONV_W ** -0.5),
        "sc_w_out": nrm(ks[8], (N_A, D, D), D ** -0.5),
        "kv_norm_g": gain(ks[9], (D,)),
        "w_kv": nrm(ks[10], (D, 2 * D), D ** -0.5),
        "k_norm_g": gain(ks[11], (hd,)),
        "w_f": nrm(ks[12], (D, H), 0.5 * D ** -0.5),
        "b_f": jax.random.uniform(ks[13], (H,), jnp.float32, 1.0, 6.0),
        "q_w": nrm(ks[14], (N_B, D, D), D ** -0.5),
        "q_norm_g": gain(ks[15], (N_B, hd)),
        "o_w": nrm(ks[16], (N_B, D, D), D ** -0.5),
        "mlp_w1": nrm(ks[17], (DEPTH, D, FF), D ** -0.5),
        "mlp_w2": nrm(ks[18], (DEPTH, FF, D), FF ** -0.5),
    }


def reference(x, c, ada_w, ada_b, norm_mix_g, norm_mlp_g, sc_w_in, sc_conv, sc_w_out,
              kv_norm_g, w_kv, k_norm_g, w_f, b_f, q_w, q_norm_g, o_w, mlp_w1, mlp_w2):
    c_act = jax.nn.silu(c)
    k = v = F = None
    for l in range(DEPTH):
        mod = c_act @ ada_w[l] + ada_b[l]
        sh_m, sc_m, g_m, sh_f, sc_f, g_f = jnp.split(mod, N_MOD, axis=-1)
        h = modulate(rms_norm(x, norm_mix_g[l]), sh_m, sc_m)
        if l < N_A:
            y = short_conv_mixer(h, sc_w_in[l], sc_conv[l], sc_w_out[l])
        else:
            i = l - N_A
            y = fox_mixer(h, q_w[i], q_norm_g[i], o_w[i], k, v, F)
        x = x + g_m[:, None, :] * y
        h = modulate(rms_norm(x, norm_mlp_g[l]), sh_f, sc_f)
        x = x + g_f[:, None, :] * squared_relu_mlp(h, mlp_w1[l], mlp_w2[l])
        if l == N_A - 1:
            k, v, F = shared_kv(x, kv_norm_g, w_kv, k_norm_g, w_f, b_f)
    return x
```

```python
import functools

import numpy as np
import jax
import jax.numpy as jnp
from jax import lax
from jax.experimental import pallas as pl
from jax.experimental.pallas import tpu as pltpu

F32 = jnp.float32
BF16 = jnp.bfloat16

N_HEADS = 16
HEAD_DIM = 64
SLAB = 2 * HEAD_DIM
N_MOD = 6
CONV_W = 3
EPS = 1e-6
MASK_VALUE = -1e30

TOKEN_TILE = 512
FF_CHUNK = 1024
Q_TILE = 256
KV_TILE = 256
ADA_COLS = 1536
VMEM_LIMIT_BYTES = 56 * 1024 * 1024

NT_DIMS = (((1,), (1,)), ((), ()))


def _cparams(n_axes):
    return pltpu.CompilerParams(
        dimension_semantics=("arbitrary",) * n_axes,
        vmem_limit_bytes=VMEM_LIMIT_BYTES)


def _resident(block_shape, index_map):
    return pl.BlockSpec(block_shape, index_map, pipeline_mode=pl.Buffered(1))


def _rms_scale(x):
    ms = jnp.mean(x * x, axis=-1, keepdims=True)
    return x * lax.rsqrt(ms + EPS)


def _norm_mod(x, g, shift, scale):
    return (_rms_scale(x) * g) * (1.0 + scale) + shift


def _split3(v):
    hi = v.astype(BF16).astype(F32)
    r = v - hi
    mid = r.astype(BF16).astype(F32)
    lo = (r - mid).astype(BF16).astype(F32)
    return hi, mid, lo


def _ada_kernel(c_ref, w_ref, b_ref, o_ref):
    c = c_ref[...]
    ca = c * jax.nn.sigmoid(c)
    o_ref[0] = jnp.dot(ca.astype(BF16), w_ref[0].astype(BF16),
                       preferred_element_type=F32) + b_ref[0]


def _ada_call(c, ada_w, ada_b):
    depth, d, nm = ada_w.shape
    b = c.shape[0]
    rows = 8
    c_pad = jnp.zeros((rows, d), F32).at[:b].set(c)
    out = pl.pallas_call(
        _ada_kernel,
        grid=(depth, nm // ADA_COLS),
        in_specs=[
            pl.BlockSpec((rows, d), lambda l, n: (0, 0)),
            pl.BlockSpec((1, d, ADA_COLS), lambda l, n: (l, 0, n)),
            pl.BlockSpec((1, 1, ADA_COLS), lambda l, n: (l, 0, n)),
        ],
        out_specs=pl.BlockSpec((1, rows, ADA_COLS), lambda l, n: (l, 0, n)),
        out_shape=jax.ShapeDtypeStruct((depth, rows, nm), F32),
        compiler_params=_cparams(2),
        name="ada_mod",
    )(c_pad, ada_w, ada_b.reshape(depth, 1, nm))
    return out[:, :b, :].reshape(depth, b, N_MOD, 1, d)


def _conv_kernel(tiles_per_seq, x_ref, mod_ref, g_ref, win_ref, cw_ref, wout_ref,
                 o_ref, zbuf):
    tm, d = x_ref.shape
    t = pl.program_id(0)

    @pl.when(t % tiles_per_seq == 0)
    def _():
        zbuf[0:8, :] = jnp.zeros((8, d), F32)

    x = x_ref[...]
    h = _norm_mod(x, g_ref[...], mod_ref[0], mod_ref[1]).astype(BF16)
    u = jnp.dot(h, win_ref[...], preferred_element_type=F32)
    z = u[:, d:2 * d] * u[:, 2 * d:]
    zbuf[8:8 + tm, :] = z
    z1 = zbuf[7:7 + tm, :]
    z2 = zbuf[6:6 + tm, :]
    zc = cw_ref[0] * z2 + cw_ref[1] * z1 + cw_ref[2] * z
    zbuf[0:8, :] = z[tm - 8:, :]
    gated = (u[:, :d] * zc).astype(BF16)
    y = jnp.dot(gated, wout_ref[...], preferred_element_type=F32)
    o_ref[...] = x + mod_ref[2] * y


def _conv_call(x, mod_l, g, w_in, conv_w, w_out, seq):
    n, d = x.shape
    tm = TOKEN_TILE
    tps = seq // tm
    return pl.pallas_call(
        functools.partial(_conv_kernel, tps),
        grid=(n // tm,),
        in_specs=[
            pl.BlockSpec((tm, d), lambda t: (t, 0)),
            pl.BlockSpec((None, N_MOD, 1, d), lambda t: (t // tps, 0, 0, 0)),
            _resident((1, d), lambda t: (0, 0)),
            _resident((d, 3 * d), lambda t: (0, 0)),
            _resident((CONV_W, 1, d), lambda t: (0, 0, 0)),
            _resident((d, d), lambda t: (0, 0)),
        ],
        out_specs=pl.BlockSpec((tm, d), lambda t: (t, 0)),
        out_shape=jax.ShapeDtypeStruct((n, d), F32),
        scratch_shapes=[pltpu.VMEM((tm + 8, d), F32)],
        compiler_params=_cparams(1),
        name="conv_mixer",
    )(x, mod_l, g.reshape(1, d), w_in, conv_w.reshape(CONV_W, 1, d), w_out)


def _mlp_residual(x1, mod_ref, g, w1_ref, w2_ref):
    d, ff = w1_ref.shape
    h = _norm_mod(x1, g, mod_ref[3], mod_ref[4]).astype(BF16)
    acc = jnp.zeros(x1.shape, F32)
    for c in range(ff // FF_CHUNK):
        cols = slice(c * FF_CHUNK, (c + 1) * FF_CHUNK)
        a = jnp.maximum(jnp.dot(h, w1_ref[:, cols], preferred_element_type=F32), 0.0)
        acc = acc + jnp.dot((a * a).astype(BF16), w2_ref[cols, :],
                            preferred_element_type=F32)
    return x1 + mod_ref[5] * acc


def _mlp_kernel(x_ref, mod_ref, g_ref, w1_ref, w2_ref, o_ref):
    o_ref[...] = _mlp_residual(x_ref[...], mod_ref, g_ref[...], w1_ref, w2_ref)


def _omlp_kernel(x_ref, a_ref, mod_ref, g_ref, ow_ref, w1_ref, w2_ref, o_ref):
    y = jnp.dot(a_ref[...], ow_ref[...], preferred_element_type=F32)
    x1 = x_ref[...] + mod_ref[2] * y
    o_ref[...] = _mlp_residual(x1, mod_ref, g_ref[...], w1_ref, w2_ref)


def _mlp_call(x, mod_l, g, w1, w2, seq, attn=None, o_w=None):
    n, d = x.shape
    ff = w1.shape[1]
    tm = TOKEN_TILE
    tps = seq // tm
    tile = pl.BlockSpec((tm, d), lambda t: (t, 0))
    mod_spec = pl.BlockSpec((None, N_MOD, 1, d), lambda t: (t // tps, 0, 0, 0))
    g_spec = _resident((1, d), lambda t: (0, 0))
    w_specs = [_resident((d, ff), lambda t: (0, 0)), _resident((ff, d), lambda t: (0, 0))]
    if attn is None:
        kernel, name = _mlp_kernel, "mlp"
        in_specs = [tile, mod_spec, g_spec] + w_specs
        args = (x, mod_l, g.reshape(1, d), w1, w2)
    else:
        kernel, name = _omlp_kernel, "oproj_mlp"
        in_specs = [tile, tile, mod_spec, g_spec, _resident((d, d), lambda t: (0, 0))] + w_specs
        args = (x, attn, mod_l, g.reshape(1, d), o_w, w1, w2)
    return pl.pallas_call(
        kernel,
        grid=(n // tm,),
        in_specs=in_specs,
        out_specs=tile,
        out_shape=jax.ShapeDtypeStruct((n, d), F32),
        compiler_params=_cparams(1),
        name=name,
    )(*args)


def _extra_base(head):
    return HEAD_DIM if head % 2 == 0 else 0


ONE_LANE = SLAB - 1


def _selection_matrices():
    sel_k = np.zeros((SLAB, N_HEADS * SLAB), np.float32)
    sel_q = np.zeros((N_HEADS * HEAD_DIM, SLAB), np.float32)
    for h in range(N_HEADS):
        base = h * SLAB + _extra_base(h)
        for piece in range(3):
            sel_q[h * HEAD_DIM + piece, piece * N_HEADS + h] = 1.0
            sel_q[h * HEAD_DIM + 3 + piece, ONE_LANE] = 1.0
            sel_k[ONE_LANE, base + piece] = 1.0
            sel_k[piece * N_HEADS + h, base + 3 + piece] = -1.0
    return jnp.asarray(sel_k, BF16), jnp.asarray(sel_q, BF16)


def _pieces(f, index):
    hi, mid, lo = _split3(f)
    one = jnp.where(index == ONE_LANE, 1.0, 0.0)
    return jnp.where(index < N_HEADS, hi,
                     jnp.where(index < 2 * N_HEADS, mid,
                               jnp.where(index < 3 * N_HEADS, lo, one))).astype(BF16)


def _kv_kernel(tiles_per_seq, x_ref, g_ref, wk_ref, wvT_ref, wf_ref, bf_ref, kg_ref,
               selk_ref, tri_ref, k_ref, vT_ref, fT_ref, carry):
    tm, d = x_ref.shape
    kv_tile = vT_ref.shape[-1]
    t = pl.program_id(0)

    @pl.when(t % tiles_per_seq == 0)
    def _():
        carry[...] = jnp.zeros(carry.shape, F32)

    h = (_rms_scale(x_ref[...]) * g_ref[...]).astype(BF16)

    xf = jnp.dot(h, wf_ref[...], preferred_element_type=F32) + bf_ref[...]
    log_f = jnp.minimum(xf, 0.0) - jnp.log1p(jnp.exp(-jnp.abs(xf)))
    tri = tri_ref[...]
    f = carry[0:1, :]
    for piece in _split3(log_f):
        f = f + jnp.dot(tri, piece.astype(BF16), preferred_element_type=F32)
    carry[...] = jnp.broadcast_to(f[tm - 1:tm, :], carry.shape)
    fT_ref[0] = f.T

    lane = lax.broadcasted_iota(jnp.int32, (tm, SLAB), 1)
    extras = jnp.dot(_pieces(f, lane), selk_ref[...], preferred_element_type=F32)

    k = jnp.dot(h, wk_ref[...], preferred_element_type=F32)
    low = lane < HEAD_DIM
    for j in range(N_HEADS // 2):
        ks = k[:, j * SLAB:(j + 1) * SLAB]
        k2 = ks * ks
        ms_lo = jnp.sum(jnp.where(low, k2, 0.0), axis=-1, keepdims=True) * (1.0 / HEAD_DIM)
        ms_hi = jnp.sum(jnp.where(low, 0.0, k2), axis=-1, keepdims=True) * (1.0 / HEAD_DIM)
        kn = ks * jnp.where(low, lax.rsqrt(ms_lo + EPS), lax.rsqrt(ms_hi + EPS)) * kg_ref[...]
        ex_even = extras[:, (2 * j) * SLAB:(2 * j + 1) * SLAB]
        ex_odd = extras[:, (2 * j + 1) * SLAB:(2 * j + 2) * SLAB]
        k_ref[0, 2 * j] = jnp.where(low, kn, ex_even).astype(BF16)
        k_ref[0, 2 * j + 1] = jnp.where(low, ex_odd, kn).astype(BF16)

    vT = lax.dot_general(wvT_ref[...], h, NT_DIMS, preferred_element_type=F32)
    row = lax.broadcasted_iota(jnp.int32, (HEAD_DIM, kv_tile), 0)
    ones_blk = jnp.where(row == 0, 1.0, 0.0).astype(BF16)
    for hd in range(N_HEADS):
        v_rows = vT[hd * HEAD_DIM:(hd + 1) * HEAD_DIM, :].astype(BF16)
        data = slice(0, HEAD_DIM) if hd % 2 == 0 else slice(HEAD_DIM, SLAB)
        other = slice(HEAD_DIM, SLAB) if hd % 2 == 0 else slice(0, HEAD_DIM)
        for c in range(tm // kv_tile):
            vT_ref[0, hd, c, data, :] = v_rows[:, c * kv_tile:(c + 1) * kv_tile]
            vT_ref[0, hd, c, other, :] = ones_blk


def _kv_call(x, g, w_k, w_vT, w_f3, b_f3, k_g2, sel_k, batch, seq):
    n, d = x.shape
    tm = TOKEN_TILE
    tps = seq // tm
    cpt = tm // KV_TILE
    tri = jnp.asarray(np.tril(np.ones((tm, tm), np.float32)), BF16)
    return pl.pallas_call(
        functools.partial(_kv_kernel, tps),
        grid=(n // tm,),
        in_specs=[
            pl.BlockSpec((tm, d), lambda t: (t, 0)),
            _resident((1, d), lambda t: (0, 0)),
            _resident((d, d), lambda t: (0, 0)),
            _resident((d, d), lambda t: (0, 0)),
            _resident((d, SLAB), lambda t: (0, 0)),
            _resident((1, SLAB), lambda t: (0, 0)),
            _resident((1, SLAB), lambda t: (0, 0)),
            _resident((SLAB, N_HEADS * SLAB), lambda t: (0, 0)),
            _resident((tm, tm), lambda t: (0, 0)),
        ],
        out_specs=[
            pl.BlockSpec((1, N_HEADS, tm, SLAB), lambda t: (t // tps, 0, t % tps, 0)),
            pl.BlockSpec((1, N_HEADS, cpt, SLAB, KV_TILE),
                         lambda t: (t // tps, 0, t % tps, 0, 0)),
            pl.BlockSpec((1, SLAB, tm), lambda t: (t // tps, 0, t % tps)),
        ],
        out_shape=[
            jax.ShapeDtypeStruct((batch, N_HEADS, seq, SLAB), BF16),
            jax.ShapeDtypeStruct((batch, N_HEADS, seq // KV_TILE, SLAB, KV_TILE), BF16),
            jax.ShapeDtypeStruct((batch, SLAB, seq), F32),
        ],
        scratch_shapes=[pltpu.VMEM((8, SLAB), F32)],
        compiler_params=_cparams(1),
        name="shared_kv",
    )(x, g.reshape(1, d), w_k, w_vT, w_f3, b_f3, k_g2, sel_k, tri)


def _q_kernel(x_ref, mod_ref, g_ref, qwT_ref, qg_ref, fT_ref, selq_ref, o_ref):
    tm, d = x_ref.shape
    h = _norm_mod(x_ref[...], g_ref[...], mod_ref[0], mod_ref[1]).astype(BF16)
    qT = lax.dot_general(qwT_ref[...], h, NT_DIMS, preferred_element_type=F32)
    row = lax.broadcasted_iota(jnp.int32, (SLAB, tm), 0)
    extras = jnp.dot(selq_ref[...], _pieces(fT_ref[0], row), preferred_element_type=F32)
    for hd in range(N_HEADS):
        rows = slice(hd * HEAD_DIM, (hd + 1) * HEAD_DIM)
        qh = qT[rows, :]
        ms = jnp.mean(qh * qh, axis=0, keepdims=True)
        qn = qh * lax.rsqrt(ms + EPS) * qg_ref[...]
        data = slice(0, HEAD_DIM) if hd % 2 == 0 else slice(HEAD_DIM, SLAB)
        other = slice(HEAD_DIM, SLAB) if hd % 2 == 0 else slice(0, HEAD_DIM)
        o_ref[0, hd, data, :] = qn.astype(BF16)
        o_ref[0, hd, other, :] = extras[rows, :].astype(BF16)


def _q_call(x, mod_l, g, q_wT, q_g_cols, fT, sel_q, batch, seq):
    n, d = x.shape
    tm = TOKEN_TILE
    tps = seq // tm
    return pl.pallas_call(
        _q_kernel,
        grid=(n // tm,),
        in_specs=[
            pl.BlockSpec((tm, d), lambda t: (t, 0)),
            pl.BlockSpec((None, N_MOD, 1, d), lambda t: (t // tps, 0, 0, 0)),
            _resident((1, d), lambda t: (0, 0)),
            _resident((d, d), lambda t: (0, 0)),
            _resident((HEAD_DIM, tm), lambda t: (0, 0)),
            pl.BlockSpec((1, SLAB, tm), lambda t: (t // tps, 0, t % tps)),
            _resident((N_HEADS * HEAD_DIM, SLAB), lambda t: (0, 0)),
        ],
        out_specs=pl.BlockSpec((1, N_HEADS, SLAB, tm), lambda t: (t // tps, 0, 0, t % tps)),
        out_shape=jax.ShapeDtypeStruct((batch, N_HEADS, SLAB, seq), BF16),
        compiler_params=_cparams(1),
        name="fox_query",
    )(x, mod_l, g.reshape(1, d), q_wT, q_g_cols, fT, sel_q)


def _attn_kernel(qT_ref, k_ref, vT_ref, o_ref, acc_ref, m_ref):
    tq = qT_ref.shape[-1]
    tk = vT_ref.shape[-1]
    i = pl.program_id(2)

    for hh in range(2):
        acc_ref[hh] = jnp.zeros((SLAB, tq), F32)
        m_ref[hh] = jnp.full((8, tq), MASK_VALUE, F32)

    def step(kb, masked):
        start = pl.multiple_of(kb * tk, tk)
        for hh in range(2):
            s = jnp.dot(k_ref[0, hh, pl.ds(start, tk), :], qT_ref[0, hh],
                        preferred_element_type=F32)
            if masked:
                key_pos = kb * tk + lax.broadcasted_iota(jnp.int32, (tk, tq), 0)
                q_pos = i * tq + lax.broadcasted_iota(jnp.int32, (tk, tq), 1)
                s = jnp.where(key_pos <= q_pos, s, MASK_VALUE)
            m_old = m_ref[hh, 0:1, :]
            m_new = jnp.maximum(m_old, jnp.max(s, axis=0, keepdims=True))
            alpha = jnp.exp(m_old - m_new)
            p = jnp.exp(s - m_new).astype(BF16)
            pv = jnp.dot(vT_ref[0, hh, kb], p, preferred_element_type=F32)
            acc_ref[hh] = alpha * acc_ref[hh] + pv
            m_ref[hh] = jnp.broadcast_to(m_new, (8, tq))

    def body(kb, carry):
        step(kb, False)
        return carry

    lax.fori_loop(0, i, body, 0)
    step(i, True)

    acc0 = acc_ref[0]
    acc1 = acc_ref[1]
    out0 = (acc0 * (1.0 / acc0[HEAD_DIM:HEAD_DIM + 1, :])).T
    out1 = (acc1 * (1.0 / acc1[0:1, :])).T
    lane = lax.broadcasted_iota(jnp.int32, (tq, SLAB), 1)
    o_ref[0] = jnp.where(lane < HEAD_DIM, out0, out1).astype(BF16)


def _attn_call(qT, k, vT, batch, seq):
    assert Q_TILE == KV_TILE
    nkb = seq // KV_TILE
    pairs = N_HEADS // 2
    return pl.pallas_call(
        _attn_kernel,
        grid=(batch, pairs, seq // Q_TILE),
        in_specs=[
            pl.BlockSpec((1, 2, SLAB, Q_TILE), lambda b, j, i: (b, j, 0, i)),
            pl.BlockSpec((1, 2, seq, SLAB), lambda b, j, i: (b, j, 0, 0)),
            pl.BlockSpec((1, 2, nkb, SLAB, KV_TILE), lambda b, j, i: (b, j, 0, 0, 0)),
        ],
        out_specs=pl.BlockSpec((1, Q_TILE, SLAB), lambda b, j, i: (b, i, j)),
        out_shape=jax.ShapeDtypeStruct((batch, seq, N_HEADS * HEAD_DIM), BF16),
        scratch_shapes=[pltpu.VMEM((2, SLAB, Q_TILE), F32), pltpu.VMEM((2, 8, Q_TILE), F32)],
        compiler_params=_cparams(3),
        name="fox_attention",
    )(qT, k, vT)


def kernel(x, c, ada_w, ada_b, norm_mix_g, norm_mlp_g, sc_w_in, sc_conv, sc_w_out,
           kv_norm_g, w_kv, k_norm_g, w_f, b_f, q_w, q_norm_g, o_w, mlp_w1, mlp_w2):
    batch, seq, d = x.shape
    depth = ada_w.shape[0]
    n_conv = sc_w_in.shape[0]
    assert d == N_HEADS * HEAD_DIM and seq % TOKEN_TILE == 0 and TOKEN_TILE % KV_TILE == 0

    mod = _ada_call(c, ada_w, ada_b)
    sel_k, sel_q = _selection_matrices()
    scale = 1.0 / np.sqrt(HEAD_DIM)

    xs = x.reshape(batch * seq, d)
    k = vT = fT = None
    for l in range(depth):
        if l < n_conv:
            xs = _conv_call(xs, mod[l], norm_mix_g[l], sc_w_in[l].astype(BF16), sc_conv[l],
                            sc_w_out[l].astype(BF16), seq)
            xs = _mlp_call(xs, mod[l], norm_mlp_g[l], mlp_w1[l].astype(BF16),
                           mlp_w2[l].astype(BF16), seq)
        else:
            i = l - n_conv
            q_g_cols = jnp.broadcast_to((q_norm_g[i] * scale)[:, None], (HEAD_DIM, TOKEN_TILE))
            qT = _q_call(xs, mod[l], norm_mix_g[l], q_w[i].T.astype(BF16), q_g_cols, fT, sel_q,
                         batch, seq)
            attn = _attn_call(qT, k, vT, batch, seq).reshape(batch * seq, d)
            xs = _mlp_call(xs, mod[l], norm_mlp_g[l], mlp_w1[l].astype(BF16),
                           mlp_w2[l].astype(BF16), seq, attn=attn, o_w=o_w[i].astype(BF16))
        if l == n_conv - 1:
            w_f3 = jnp.zeros((d, SLAB), F32).at[:, :3 * N_HEADS].set(jnp.tile(w_f, (1, 3)))
            b_f3 = jnp.zeros((1, SLAB), F32).at[0, :3 * N_HEADS].set(jnp.tile(b_f, 3))
            k, vT, fT = _kv_call(xs, kv_norm_g, w_kv[:, :d].astype(BF16),
                                 w_kv[:, d:].T.astype(BF16), w_f3.astype(BF16), b_f3,
                                 jnp.tile(k_norm_g, 2).reshape(1, SLAB), sel_k, batch, seq)
    return xs.reshape(batch, seq, d)
```

```python
import functools

import numpy as np
import jax
import jax.numpy as jnp
from jax import lax
from jax.experimental import pallas as pl
from jax.experimental.pallas import tpu as pltpu

F32 = jnp.float32
BF16 = jnp.bfloat16

N_HEADS = 16
HEAD_DIM = 64
SLAB = 2 * HEAD_DIM
N_MOD = 6
CONV_W = 3
EPS = 1e-6
MASK_VALUE = -1e30

TOKEN_TILE = 512
FF_CHUNK = 1024
Q_TILE = 256
KV_TILE = 1024
ADA_COLS = 1536
LOG2E = 1.4426950408889634
VMEM_LIMIT_BYTES = 56 * 1024 * 1024

NT_DIMS = (((1,), (1,)), ((), ()))


def _cparams(n_axes):
    return pltpu.CompilerParams(
        dimension_semantics=("arbitrary",) * n_axes,
        vmem_limit_bytes=VMEM_LIMIT_BYTES)


def _resident(block_shape, index_map):
    return pl.BlockSpec(block_shape, index_map, pipeline_mode=pl.Buffered(1))


def _rms_scale(x):
    ms = jnp.mean(x * x, axis=-1, keepdims=True)
    return x * lax.rsqrt(ms + EPS)


def _norm_mod(x, g, shift, scale):
    return (_rms_scale(x) * g) * (1.0 + scale) + shift


def _split3(v):
    hi = v.astype(BF16).astype(F32)
    r = v - hi
    mid = r.astype(BF16).astype(F32)
    lo = (r - mid).astype(BF16).astype(F32)
    return hi, mid, lo


def _ada_kernel(c_ref, w_ref, b_ref, o_ref):
    c = c_ref[...]
    ca = c * jax.nn.sigmoid(c)
    o_ref[0] = jnp.dot(ca.astype(BF16), w_ref[0].astype(BF16),
                       preferred_element_type=F32) + b_ref[0]


def _ada_call(c, ada_w, ada_b):
    depth, d, nm = ada_w.shape
    b = c.shape[0]
    rows = 8
    c_pad = jnp.zeros((rows, d), F32).at[:b].set(c)
    out = pl.pallas_call(
        _ada_kernel,
        grid=(depth, nm // ADA_COLS),
        in_specs=[
            pl.BlockSpec((rows, d), lambda l, n: (0, 0)),
            pl.BlockSpec((1, d, ADA_COLS), lambda l, n: (l, 0, n)),
            pl.BlockSpec((1, 1, ADA_COLS), lambda l, n: (l, 0, n)),
        ],
        out_specs=pl.BlockSpec((1, rows, ADA_COLS), lambda l, n: (l, 0, n)),
        out_shape=jax.ShapeDtypeStruct((depth, rows, nm), F32),
        compiler_params=_cparams(2),
        name="ada_mod",
    )(c_pad, ada_w, ada_b.reshape(depth, 1, nm))
    return out[:, :b, :].reshape(depth, b, N_MOD, 1, d)


def _conv_kernel(tiles_per_seq, x_ref, mod_ref, g_ref, win_ref, cw_ref, wout_ref,
                 o_ref, zbuf):
    tm, d = x_ref.shape
    t = pl.program_id(0)

    @pl.when(t % tiles_per_seq == 0)
    def _():
        zbuf[0:8, :] = jnp.zeros((8, d), F32)

    x = x_ref[...]
    h = _norm_mod(x, g_ref[...], mod_ref[0], mod_ref[1]).astype(BF16)
    u = jnp.dot(h, win_ref[...], preferred_element_type=F32)
    z = u[:, d:2 * d] * u[:, 2 * d:]
    zbuf[8:8 + tm, :] = z
    z1 = zbuf[7:7 + tm, :]
    z2 = zbuf[6:6 + tm, :]
    zc = cw_ref[0] * z2 + cw_ref[1] * z1 + cw_ref[2] * z
    zbuf[0:8, :] = z[tm - 8:, :]
    gated = (u[:, :d] * zc).astype(BF16)
    y = jnp.dot(gated, wout_ref[...], preferred_element_type=F32)
    o_ref[...] = x + mod_ref[2] * y


def _conv_call(x, mod_l, g, w_in, conv_w, w_out, seq):
    n, d = x.shape
    tm = TOKEN_TILE
    tps = seq // tm
    return pl.pallas_call(
        functools.partial(_conv_kernel, tps),
        grid=(n // tm,),
        in_specs=[
            pl.BlockSpec((tm, d), lambda t: (t, 0)),
            pl.BlockSpec((None, N_MOD, 1, d), lambda t: (t // tps, 0, 0, 0)),
            _resident((1, d), lambda t: (0, 0)),
            _resident((d, 3 * d), lambda t: (0, 0)),
            _resident((CONV_W, 1, d), lambda t: (0, 0, 0)),
            _resident((d, d), lambda t: (0, 0)),
        ],
        out_specs=pl.BlockSpec((tm, d), lambda t: (t, 0)),
        out_shape=jax.ShapeDtypeStruct((n, d), F32),
        scratch_shapes=[pltpu.VMEM((tm + 8, d), F32)],
        compiler_params=_cparams(1),
        name="conv_mixer",
    )(x, mod_l, g.reshape(1, d), w_in, conv_w.reshape(CONV_W, 1, d), w_out)


def _mlp_residual(x1, mod_ref, g, w1_ref, w2_ref):
    d, ff = w1_ref.shape
    h = _norm_mod(x1, g, mod_ref[3], mod_ref[4]).astype(BF16)
    acc = jnp.zeros(x1.shape, F32)
    for c in range(ff // FF_CHUNK):
        cols = slice(c * FF_CHUNK, (c + 1) * FF_CHUNK)
        a = jnp.maximum(jnp.dot(h, w1_ref[:, cols], preferred_element_type=F32), 0.0)
        acc = acc + jnp.dot((a * a).astype(BF16), w2_ref[cols, :],
                            preferred_element_type=F32)
    return x1 + mod_ref[5] * acc


def _mlp_kernel(x_ref, mod_ref, g_ref, w1_ref, w2_ref, o_ref):
    o_ref[...] = _mlp_residual(x_ref[...], mod_ref, g_ref[...], w1_ref, w2_ref)


def _omlp_kernel(x_ref, a_ref, mod_ref, g_ref, ow_ref, w1_ref, w2_ref, o_ref):
    y = jnp.dot(a_ref[...], ow_ref[...], preferred_element_type=F32)
    x1 = x_ref[...] + mod_ref[2] * y
    o_ref[...] = _mlp_residual(x1, mod_ref, g_ref[...], w1_ref, w2_ref)


def _mlp_call(x, mod_l, g, w1, w2, seq, attn=None, o_w=None):
    n, d = x.shape
    ff = w1.shape[1]
    tm = TOKEN_TILE
    tps = seq // tm
    tile = pl.BlockSpec((tm, d), lambda t: (t, 0))
    mod_spec = pl.BlockSpec((None, N_MOD, 1, d), lambda t: (t // tps, 0, 0, 0))
    g_spec = _resident((1, d), lambda t: (0, 0))
    w_specs = [_resident((d, ff), lambda t: (0, 0)), _resident((ff, d), lambda t: (0, 0))]
    if attn is None:
        kernel, name = _mlp_kernel, "mlp"
        in_specs = [tile, mod_spec, g_spec] + w_specs
        args = (x, mod_l, g.reshape(1, d), w1, w2)
    else:
        kernel, name = _omlp_kernel, "oproj_mlp"
        in_specs = [tile, tile, mod_spec, g_spec, _resident((d, d), lambda t: (0, 0))] + w_specs
        args = (x, attn, mod_l, g.reshape(1, d), o_w, w1, w2)
    return pl.pallas_call(
        kernel,
        grid=(n // tm,),
        in_specs=in_specs,
        out_specs=tile,
        out_shape=jax.ShapeDtypeStruct((n, d), F32),
        compiler_params=_cparams(1),
        name=name,
    )(*args)


def _extra_base(head):
    return HEAD_DIM if head % 2 == 0 else 0


ONE_LANE = SLAB - 1


def _selection_matrices():
    sel_k = np.zeros((SLAB, N_HEADS * SLAB), np.float32)
    sel_q = np.zeros((N_HEADS * HEAD_DIM, SLAB), np.float32)
    for h in range(N_HEADS):
        base = h * SLAB + _extra_base(h)
        for piece in range(3):
            sel_q[h * HEAD_DIM + piece, piece * N_HEADS + h] = 1.0
            sel_q[h * HEAD_DIM + 3 + piece, ONE_LANE] = 1.0
            sel_k[ONE_LANE, base + piece] = 1.0
            sel_k[piece * N_HEADS + h, base + 3 + piece] = -1.0
    return jnp.asarray(sel_k, BF16), jnp.asarray(sel_q, BF16)


def _pieces(f, index):
    hi, mid, lo = _split3(f)
    one = jnp.where(index == ONE_LANE, 1.0, 0.0)
    return jnp.where(index < N_HEADS, hi,
                     jnp.where(index < 2 * N_HEADS, mid,
                               jnp.where(index < 3 * N_HEADS, lo, one))).astype(BF16)


def _kv_kernel(tiles_per_seq, x_ref, g_ref, wk_ref, wvT_ref, wf_ref, bf_ref, kg_ref,
               selk_ref, tri_ref, k_ref, vT_ref, fT_ref, carry):
    tm, d = x_ref.shape
    t = pl.program_id(0)

    @pl.when(t % tiles_per_seq == 0)
    def _():
        carry[...] = jnp.zeros(carry.shape, F32)

    h = (_rms_scale(x_ref[...]) * g_ref[...]).astype(BF16)

    xf = jnp.dot(h, wf_ref[...], preferred_element_type=F32) + bf_ref[...]
    log_f = jnp.minimum(xf, 0.0) - jnp.log1p(jnp.exp(-jnp.abs(xf)))
    tri = tri_ref[...]
    f = carry[0:1, :]
    for piece in _split3(log_f):
        f = f + jnp.dot(tri, piece.astype(BF16), preferred_element_type=F32)
    carry[...] = jnp.broadcast_to(f[tm - 1:tm, :], carry.shape)
    f2 = f * LOG2E
    fT_ref[0] = f2.T

    lane = lax.broadcasted_iota(jnp.int32, (tm, SLAB), 1)
    extras = jnp.dot(_pieces(f2, lane), selk_ref[...], preferred_element_type=F32)

    k = jnp.dot(h, wk_ref[...], preferred_element_type=F32)
    low = lane < HEAD_DIM
    for j in range(N_HEADS // 2):
        ks = k[:, j * SLAB:(j + 1) * SLAB]
        k2 = ks * ks
        ms_lo = jnp.sum(jnp.where(low, k2, 0.0), axis=-1, keepdims=True) * (1.0 / HEAD_DIM)
        ms_hi = jnp.sum(jnp.where(low, 0.0, k2), axis=-1, keepdims=True) * (1.0 / HEAD_DIM)
        kn = ks * jnp.where(low, lax.rsqrt(ms_lo + EPS), lax.rsqrt(ms_hi + EPS)) * kg_ref[...]
        ex_even = extras[:, (2 * j) * SLAB:(2 * j + 1) * SLAB]
        ex_odd = extras[:, (2 * j + 1) * SLAB:(2 * j + 2) * SLAB]
        k_ref[0, 2 * j] = jnp.where(low, kn, ex_even).astype(BF16)
        k_ref[0, 2 * j + 1] = jnp.where(low, ex_odd, kn).astype(BF16)

    vT = lax.dot_general(wvT_ref[...], h, NT_DIMS, preferred_element_type=F32)
    row = lax.broadcasted_iota(jnp.int32, (HEAD_DIM, tm), 0)
    ones_blk = jnp.where(row == 0, 1.0, 0.0).astype(BF16)
    for hd in range(N_HEADS):
        data = slice(0, HEAD_DIM) if hd % 2 == 0 else slice(HEAD_DIM, SLAB)
        other = slice(HEAD_DIM, SLAB) if hd % 2 == 0 else slice(0, HEAD_DIM)
        vT_ref[0, hd, 0, data, :] = vT[hd * HEAD_DIM:(hd + 1) * HEAD_DIM, :].astype(BF16)
        vT_ref[0, hd, 0, other, :] = ones_blk


def _kv_call(x, g, w_k, w_vT, w_f3, b_f3, k_g2, sel_k, batch, seq):
    n, d = x.shape
    tm = TOKEN_TILE
    tps = seq // tm
    tpk = KV_TILE // tm
    tri = jnp.asarray(np.tril(np.ones((tm, tm), np.float32)), BF16)
    return pl.pallas_call(
        functools.partial(_kv_kernel, tps),
        grid=(n // tm,),
        in_specs=[
            pl.BlockSpec((tm, d), lambda t: (t, 0)),
            _resident((1, d), lambda t: (0, 0)),
            _resident((d, d), lambda t: (0, 0)),
            _resident((d, d), lambda t: (0, 0)),
            _resident((d, SLAB), lambda t: (0, 0)),
            _resident((1, SLAB), lambda t: (0, 0)),
            _resident((1, SLAB), lambda t: (0, 0)),
            _resident((SLAB, N_HEADS * SLAB), lambda t: (0, 0)),
            _resident((tm, tm), lambda t: (0, 0)),
        ],
        out_specs=[
            pl.BlockSpec((1, N_HEADS, tm, SLAB), lambda t: (t // tps, 0, t % tps, 0)),
            pl.BlockSpec((1, N_HEADS, 1, SLAB, tm),
                         lambda t: (t // tps, 0, (t % tps) // tpk, 0, t % tpk)),
            pl.BlockSpec((1, SLAB, tm), lambda t: (t // tps, 0, t % tps)),
        ],
        out_shape=[
            jax.ShapeDtypeStruct((batch, N_HEADS, seq, SLAB), BF16),
            jax.ShapeDtypeStruct((batch, N_HEADS, seq // KV_TILE, SLAB, KV_TILE), BF16),
            jax.ShapeDtypeStruct((batch, SLAB, seq), F32),
        ],
        scratch_shapes=[pltpu.VMEM((8, SLAB), F32)],
        compiler_params=_cparams(1),
        name="shared_kv",
    )(x, g.reshape(1, d), w_k, w_vT, w_f3, b_f3, k_g2, sel_k, tri)


def _q_kernel(shift_ref, x_ref, mod_ref, g_ref, qwT_ref, qg_ref, fT_ref, selq_ref, o_ref):
    tm, d = x_ref.shape
    h = _norm_mod(x_ref[...], g_ref[...], mod_ref[0], mod_ref[1]).astype(BF16)
    qT = lax.dot_general(qwT_ref[...], h, NT_DIMS, preferred_element_type=F32)
    row = lax.broadcasted_iota(jnp.int32, (SLAB, tm), 0)
    extras = jnp.dot(selq_ref[...], _pieces(fT_ref[0] - shift_ref[0], row),
                     preferred_element_type=F32)
    for hd in range(N_HEADS):
        rows = slice(hd * HEAD_DIM, (hd + 1) * HEAD_DIM)
        qh = qT[rows, :]
        ms = jnp.mean(qh * qh, axis=0, keepdims=True)
        qn = qh * lax.rsqrt(ms + EPS) * qg_ref[...]
        data = slice(0, HEAD_DIM) if hd % 2 == 0 else slice(HEAD_DIM, SLAB)
        other = slice(HEAD_DIM, SLAB) if hd % 2 == 0 else slice(0, HEAD_DIM)
        o_ref[0, hd, data, :] = qn.astype(BF16)
        o_ref[0, hd, other, :] = extras[rows, :].astype(BF16)


def _q_call(shift, x, mod_l, g, q_wT, q_g_cols, fT, sel_q, batch, seq):
    n, d = x.shape
    tm = TOKEN_TILE
    tps = seq // tm
    return pl.pallas_call(
        _q_kernel,
        grid=(n // tm,),
        in_specs=[
            pl.BlockSpec(memory_space=pltpu.SMEM),
            pl.BlockSpec((tm, d), lambda t: (t, 0)),
            pl.BlockSpec((None, N_MOD, 1, d), lambda t: (t // tps, 0, 0, 0)),
            _resident((1, d), lambda t: (0, 0)),
            _resident((d, d), lambda t: (0, 0)),
            _resident((HEAD_DIM, tm), lambda t: (0, 0)),
            pl.BlockSpec((1, SLAB, tm), lambda t: (t // tps, 0, t % tps)),
            _resident((N_HEADS * HEAD_DIM, SLAB), lambda t: (0, 0)),
        ],
        out_specs=pl.BlockSpec((1, N_HEADS, SLAB, tm), lambda t: (t // tps, 0, 0, t % tps)),
        out_shape=jax.ShapeDtypeStruct((batch, N_HEADS, SLAB, seq), BF16),
        compiler_params=_cparams(1),
        name="fox_query",
    )(shift, x, mod_l, g.reshape(1, d), q_wT, q_g_cols, fT, sel_q)


def _attn_kernel(qT_ref, k_ref, vT_ref, o_ref, acc_ref, m_ref):
    tq = qT_ref.shape[-1]
    tk = vT_ref.shape[-1]
    i = pl.program_id(2)
    n_kv = (i * tq + tq + tk - 1) // tk

    for hh in range(2):
        acc_ref[hh] = jnp.zeros((SLAB, tq), F32)
        m_ref[hh] = jnp.full((8, tq), MASK_VALUE, F32)

    def step(kb, masked):
        start = pl.multiple_of(kb * tk, tk)
        for hh in range(2):
            s = jnp.dot(k_ref[0, hh, pl.ds(start, tk), :], qT_ref[0, hh],
                        preferred_element_type=F32)
            if masked:
                key_pos = kb * tk + lax.broadcasted_iota(jnp.int32, (tk, tq), 0)
                q_pos = i * tq + lax.broadcasted_iota(jnp.int32, (tk, tq), 1)
                s = jnp.where(key_pos <= q_pos, s, MASK_VALUE)
            m_old = m_ref[hh, 0:1, :]
            m_new = jnp.maximum(m_old, jnp.max(s, axis=0, keepdims=True))
            alpha = jnp.exp2(m_old - m_new)
            p = jnp.exp2(s - m_new).astype(BF16)
            pv = jnp.dot(vT_ref[0, hh, kb], p, preferred_element_type=F32)
            acc_ref[hh] = alpha * acc_ref[hh] + pv
            m_ref[hh] = jnp.broadcast_to(m_new, (8, tq))

    def body(kb, carry):
        step(kb, False)
        return carry

    lax.fori_loop(0, n_kv - 1, body, 0)
    step(n_kv - 1, True)

    acc0 = acc_ref[0]
    acc1 = acc_ref[1]
    out0 = (acc0 * (1.0 / acc0[HEAD_DIM:HEAD_DIM + 1, :])).T
    out1 = (acc1 * (1.0 / acc1[0:1, :])).T
    lane = lax.broadcasted_iota(jnp.int32, (tq, SLAB), 1)
    o_ref[0] = jnp.where(lane < HEAD_DIM, out0, out1).astype(BF16)


def _attn_call(qT, k, vT, batch, seq):
    nkb = seq // KV_TILE
    pairs = N_HEADS // 2
    return pl.pallas_call(
        _attn_kernel,
        grid=(batch, pairs, seq // Q_TILE),
        in_specs=[
            pl.BlockSpec((1, 2, SLAB, Q_TILE), lambda b, j, i: (b, j, 0, i)),
            pl.BlockSpec((1, 2, seq, SLAB), lambda b, j, i: (b, j, 0, 0)),
            pl.BlockSpec((1, 2, nkb, SLAB, KV_TILE), lambda b, j, i: (b, j, 0, 0, 0)),
        ],
        out_specs=pl.BlockSpec((1, Q_TILE, SLAB), lambda b, j, i: (b, i, j)),
        out_shape=jax.ShapeDtypeStruct((batch, seq, N_HEADS * HEAD_DIM), BF16),
        scratch_shapes=[pltpu.VMEM((2, SLAB, Q_TILE), F32), pltpu.VMEM((2, 8, Q_TILE), F32)],
        compiler_params=_cparams(3),
        name="fox_attention",
    )(qT, k, vT)


def kernel(x, c, ada_w, ada_b, norm_mix_g, norm_mlp_g, sc_w_in, sc_conv, sc_w_out,
           kv_norm_g, w_kv, k_norm_g, w_f, b_f, q_w, q_norm_g, o_w, mlp_w1, mlp_w2):
    batch, seq, d = x.shape
    depth = ada_w.shape[0]
    n_conv = sc_w_in.shape[0]
    assert d == N_HEADS * HEAD_DIM and seq % TOKEN_TILE == 0 and KV_TILE % TOKEN_TILE == 0 and seq % KV_TILE == 0

    mod = _ada_call(c, ada_w, ada_b)
    sel_k, sel_q = _selection_matrices()
    q_scale = LOG2E / np.sqrt(HEAD_DIM)
    no_shift = jnp.zeros((1,), F32)

    xs = x.reshape(batch * seq, d)
    k = vT = fT = None
    for l in range(depth):
        if l < n_conv:
            xs = _conv_call(xs, mod[l], norm_mix_g[l], sc_w_in[l].astype(BF16), sc_conv[l],
                            sc_w_out[l].astype(BF16), seq)
            xs = _mlp_call(xs, mod[l], norm_mlp_g[l], mlp_w1[l].astype(BF16),
                           mlp_w2[l].astype(BF16), seq)
        else:
            i = l - n_conv
            q_g_cols = jnp.broadcast_to((q_norm_g[i] * q_scale)[:, None], (HEAD_DIM, TOKEN_TILE))
            qT = _q_call(no_shift, xs, mod[l], norm_mix_g[l], q_w[i].T.astype(BF16), q_g_cols, fT, sel_q,
                         batch, seq)
            attn = _attn_call(qT, k, vT, batch, seq).reshape(batch * seq, d)
            xs = _mlp_call(xs, mod[l], norm_mlp_g[l], mlp_w1[l].astype(BF16),
                           mlp_w2[l].astype(BF16), seq, attn=attn, o_w=o_w[i].astype(BF16))
        if l == n_conv - 1:
            w_f3 = jnp.zeros((d, SLAB), F32).at[:, :3 * N_HEADS].set(jnp.tile(w_f, (1, 3)))
            b_f3 = jnp.zeros((1, SLAB), F32).at[0, :3 * N_HEADS].set(jnp.tile(b_f, 3))
            k, vT, fT = _kv_call(xs, kv_norm_g, w_kv[:, :d].astype(BF16),
                                 w_kv[:, d:].T.astype(BF16), w_f3.astype(BF16), b_f3,
                                 jnp.tile(k_norm_g, 2).reshape(1, SLAB), sel_k, batch, seq)
    return xs.reshape(batch, seq, d)
```

```python
import functools

import numpy as np
import jax
import jax.numpy as jnp
from jax import lax
from jax.experimental import pallas as pl
from jax.experimental.pallas import tpu as pltpu

F32 = jnp.float32
BF16 = jnp.bfloat16

N_HEADS = 16
HEAD_DIM = 64
SLAB = 2 * HEAD_DIM
N_MOD = 6
CONV_W = 3
EPS = 1e-6
MASK_VALUE = -1e30

TOKEN_TILE = 512
FF_CHUNK = 1024
Q_TILE = 256
KV_TILE = 1024
ADA_COLS = 1536
LOG2E = 1.4426950408889634
MAX_PRESHIFT = 40.0
BOUND_SLACK = 1.0 + 2.0 ** -6
VMEM_LIMIT_BYTES = 56 * 1024 * 1024

NT_DIMS = (((1,), (1,)), ((), ()))


def _cparams(n_axes):
    return pltpu.CompilerParams(
        dimension_semantics=("arbitrary",) * n_axes,
        vmem_limit_bytes=VMEM_LIMIT_BYTES)


def _resident(block_shape, index_map):
    return pl.BlockSpec(block_shape, index_map, pipeline_mode=pl.Buffered(1))


def _rms_scale(x):
    ms = jnp.mean(x * x, axis=-1, keepdims=True)
    return x * lax.rsqrt(ms + EPS)


def _norm_mod(x, g, shift, scale):
    return (_rms_scale(x) * g) * (1.0 + scale) + shift


def _split3(v):
    hi = v.astype(BF16).astype(F32)
    r = v - hi
    mid = r.astype(BF16).astype(F32)
    lo = (r - mid).astype(BF16).astype(F32)
    return hi, mid, lo


def _ada_kernel(c_ref, w_ref, b_ref, o_ref):
    c = c_ref[...]
    ca = c * jax.nn.sigmoid(c)
    o_ref[0] = jnp.dot(ca.astype(BF16), w_ref[0].astype(BF16),
                       preferred_element_type=F32) + b_ref[0]


def _ada_call(c, ada_w, ada_b):
    depth, d, nm = ada_w.shape
    b = c.shape[0]
    rows = 8
    c_pad = jnp.zeros((rows, d), F32).at[:b].set(c)
    out = pl.pallas_call(
        _ada_kernel,
        grid=(depth, nm // ADA_COLS),
        in_specs=[
            pl.BlockSpec((rows, d), lambda l, n: (0, 0)),
            pl.BlockSpec((1, d, ADA_COLS), lambda l, n: (l, 0, n)),
            pl.BlockSpec((1, 1, ADA_COLS), lambda l, n: (l, 0, n)),
        ],
        out_specs=pl.BlockSpec((1, rows, ADA_COLS), lambda l, n: (l, 0, n)),
        out_shape=jax.ShapeDtypeStruct((depth, rows, nm), F32),
        compiler_params=_cparams(2),
        name="ada_mod",
    )(c_pad, ada_w, ada_b.reshape(depth, 1, nm))
    return out[:, :b, :].reshape(depth, b, N_MOD, 1, d)


def _conv_kernel(tiles_per_seq, x_ref, mod_ref, g_ref, win_ref, cw_ref, wout_ref,
                 o_ref, zbuf):
    tm, d = x_ref.shape
    t = pl.program_id(0)

    @pl.when(t % tiles_per_seq == 0)
    def _():
        zbuf[0:8, :] = jnp.zeros((8, d), F32)

    x = x_ref[...]
    h = _norm_mod(x, g_ref[...], mod_ref[0], mod_ref[1]).astype(BF16)
    u = jnp.dot(h, win_ref[...], preferred_element_type=F32)
    z = u[:, d:2 * d] * u[:, 2 * d:]
    zbuf[8:8 + tm, :] = z
    z1 = zbuf[7:7 + tm, :]
    z2 = zbuf[6:6 + tm, :]
    zc = cw_ref[0] * z2 + cw_ref[1] * z1 + cw_ref[2] * z
    zbuf[0:8, :] = z[tm - 8:, :]
    gated = (u[:, :d] * zc).astype(BF16)
    y = jnp.dot(gated, wout_ref[...], preferred_element_type=F32)
    o_ref[...] = x + mod_ref[2] * y


def _conv_call(x, mod_l, g, w_in, conv_w, w_out, seq):
    n, d = x.shape
    tm = TOKEN_TILE
    tps = seq // tm
    return pl.pallas_call(
        functools.partial(_conv_kernel, tps),
        grid=(n // tm,),
        in_specs=[
            pl.BlockSpec((tm, d), lambda t: (t, 0)),
            pl.BlockSpec((None, N_MOD, 1, d), lambda t: (t // tps, 0, 0, 0)),
            _resident((1, d), lambda t: (0, 0)),
            _resident((d, 3 * d), lambda t: (0, 0)),
            _resident((CONV_W, 1, d), lambda t: (0, 0, 0)),
            _resident((d, d), lambda t: (0, 0)),
        ],
        out_specs=pl.BlockSpec((tm, d), lambda t: (t, 0)),
        out_shape=jax.ShapeDtypeStruct((n, d), F32),
        scratch_shapes=[pltpu.VMEM((tm + 8, d), F32)],
        compiler_params=_cparams(1),
        name="conv_mixer",
    )(x, mod_l, g.reshape(1, d), w_in, conv_w.reshape(CONV_W, 1, d), w_out)


def _mlp_residual(x1, mod_ref, g, w1_ref, w2_ref):
    d, ff = w1_ref.shape
    h = _norm_mod(x1, g, mod_ref[3], mod_ref[4]).astype(BF16)
    acc = jnp.zeros(x1.shape, F32)
    for c in range(ff // FF_CHUNK):
        cols = slice(c * FF_CHUNK, (c + 1) * FF_CHUNK)
        a = jnp.maximum(jnp.dot(h, w1_ref[:, cols], preferred_element_type=F32), 0.0)
        acc = acc + jnp.dot((a * a).astype(BF16), w2_ref[cols, :],
                            preferred_element_type=F32)
    return x1 + mod_ref[5] * acc


def _mlp_kernel(x_ref, mod_ref, g_ref, w1_ref, w2_ref, o_ref):
    o_ref[...] = _mlp_residual(x_ref[...], mod_ref, g_ref[...], w1_ref, w2_ref)


def _omlp_kernel(x_ref, a_ref, mod_ref, g_ref, ow_ref, w1_ref, w2_ref, o_ref):
    y = jnp.dot(a_ref[...], ow_ref[...], preferred_element_type=F32)
    x1 = x_ref[...] + mod_ref[2] * y
    o_ref[...] = _mlp_residual(x1, mod_ref, g_ref[...], w1_ref, w2_ref)


def _mlp_call(x, mod_l, g, w1, w2, seq, attn=None, o_w=None):
    n, d = x.shape
    ff = w1.shape[1]
    tm = TOKEN_TILE
    tps = seq // tm
    tile = pl.BlockSpec((tm, d), lambda t: (t, 0))
    mod_spec = pl.BlockSpec((None, N_MOD, 1, d), lambda t: (t // tps, 0, 0, 0))
    g_spec = _resident((1, d), lambda t: (0, 0))
    w_specs = [_resident((d, ff), lambda t: (0, 0)), _resident((ff, d), lambda t: (0, 0))]
    if attn is None:
        kernel, name = _mlp_kernel, "mlp"
        in_specs = [tile, mod_spec, g_spec] + w_specs
        args = (x, mod_l, g.reshape(1, d), w1, w2)
    else:
        kernel, name = _omlp_kernel, "oproj_mlp"
        in_specs = [tile, tile, mod_spec, g_spec, _resident((d, d), lambda t: (0, 0))] + w_specs
        args = (x, attn, mod_l, g.reshape(1, d), o_w, w1, w2)
    return pl.pallas_call(
        kernel,
        grid=(n // tm,),
        in_specs=in_specs,
        out_specs=tile,
        out_shape=jax.ShapeDtypeStruct((n, d), F32),
        compiler_params=_cparams(1),
        name=name,
    )(*args)


def _extra_base(head):
    return HEAD_DIM if head % 2 == 0 else 0


ONE_LANE = SLAB - 1


def _selection_matrices():
    sel_k = np.zeros((SLAB, N_HEADS * SLAB), np.float32)
    sel_q = np.zeros((N_HEADS * HEAD_DIM, SLAB), np.float32)
    for h in range(N_HEADS):
        base = h * SLAB + _extra_base(h)
        for piece in range(3):
            sel_q[h * HEAD_DIM + piece, piece * N_HEADS + h] = 1.0
            sel_q[h * HEAD_DIM + 3 + piece, ONE_LANE] = 1.0
            sel_k[ONE_LANE, base + piece] = 1.0
            sel_k[piece * N_HEADS + h, base + 3 + piece] = -1.0
    return jnp.asarray(sel_k, BF16), jnp.asarray(sel_q, BF16)


def _pieces(f, index):
    hi, mid, lo = _split3(f)
    one = jnp.where(index == ONE_LANE, 1.0, 0.0)
    return jnp.where(index < N_HEADS, hi,
                     jnp.where(index < 2 * N_HEADS, mid,
                               jnp.where(index < 3 * N_HEADS, lo, one))).astype(BF16)


def _kv_kernel(tiles_per_seq, x_ref, g_ref, wk_ref, wvT_ref, wf_ref, bf_ref, kg_ref,
               selk_ref, tri_ref, k_ref, vT_ref, fT_ref, carry):
    tm, d = x_ref.shape
    t = pl.program_id(0)

    @pl.when(t % tiles_per_seq == 0)
    def _():
        carry[...] = jnp.zeros(carry.shape, F32)

    h = (_rms_scale(x_ref[...]) * g_ref[...]).astype(BF16)

    xf = jnp.dot(h, wf_ref[...], preferred_element_type=F32) + bf_ref[...]
    log_f = jnp.minimum(xf, 0.0) - jnp.log1p(jnp.exp(-jnp.abs(xf)))
    tri = tri_ref[...]
    f = carry[0:1, :]
    for piece in _split3(log_f):
        f = f + jnp.dot(tri, piece.astype(BF16), preferred_element_type=F32)
    carry[...] = jnp.broadcast_to(f[tm - 1:tm, :], carry.shape)
    f2 = f * LOG2E
    fT_ref[0] = f2.T

    lane = lax.broadcasted_iota(jnp.int32, (tm, SLAB), 1)
    extras = jnp.dot(_pieces(f2, lane), selk_ref[...], preferred_element_type=F32)

    k = jnp.dot(h, wk_ref[...], preferred_element_type=F32)
    low = lane < HEAD_DIM
    for j in range(N_HEADS // 2):
        ks = k[:, j * SLAB:(j + 1) * SLAB]
        k2 = ks * ks
        ms_lo = jnp.sum(jnp.where(low, k2, 0.0), axis=-1, keepdims=True) * (1.0 / HEAD_DIM)
        ms_hi = jnp.sum(jnp.where(low, 0.0, k2), axis=-1, keepdims=True) * (1.0 / HEAD_DIM)
        kn = ks * jnp.where(low, lax.rsqrt(ms_lo + EPS), lax.rsqrt(ms_hi + EPS)) * kg_ref[...]
        ex_even = extras[:, (2 * j) * SLAB:(2 * j + 1) * SLAB]
        ex_odd = extras[:, (2 * j + 1) * SLAB:(2 * j + 2) * SLAB]
        k_ref[0, 2 * j] = jnp.where(low, kn, ex_even).astype(BF16)
        k_ref[0, 2 * j + 1] = jnp.where(low, ex_odd, kn).astype(BF16)

    vT = lax.dot_general(wvT_ref[...], h, NT_DIMS, preferred_element_type=F32)
    row = lax.broadcasted_iota(jnp.int32, (HEAD_DIM, tm), 0)
    ones_blk = jnp.where(row == 0, 1.0, 0.0).astype(BF16)
    for hd in range(N_HEADS):
        data = slice(0, HEAD_DIM) if hd % 2 == 0 else slice(HEAD_DIM, SLAB)
        other = slice(HEAD_DIM, SLAB) if hd % 2 == 0 else slice(0, HEAD_DIM)
        vT_ref[0, hd, 0, data, :] = vT[hd * HEAD_DIM:(hd + 1) * HEAD_DIM, :].astype(BF16)
        vT_ref[0, hd, 0, other, :] = ones_blk


def _kv_call(x, g, w_k, w_vT, w_f3, b_f3, k_g2, sel_k, batch, seq):
    n, d = x.shape
    tm = TOKEN_TILE
    tps = seq // tm
    tpk = KV_TILE // tm
    tri = jnp.asarray(np.tril(np.ones((tm, tm), np.float32)), BF16)
    return pl.pallas_call(
        functools.partial(_kv_kernel, tps),
        grid=(n // tm,),
        in_specs=[
            pl.BlockSpec((tm, d), lambda t: (t, 0)),
            _resident((1, d), lambda t: (0, 0)),
            _resident((d, d), lambda t: (0, 0)),
            _resident((d, d), lambda t: (0, 0)),
            _resident((d, SLAB), lambda t: (0, 0)),
            _resident((1, SLAB), lambda t: (0, 0)),
            _resident((1, SLAB), lambda t: (0, 0)),
            _resident((SLAB, N_HEADS * SLAB), lambda t: (0, 0)),
            _resident((tm, tm), lambda t: (0, 0)),
        ],
        out_specs=[
            pl.BlockSpec((1, N_HEADS, tm, SLAB), lambda t: (t // tps, 0, t % tps, 0)),
            pl.BlockSpec((1, N_HEADS, 1, SLAB, tm),
                         lambda t: (t // tps, 0, (t % tps) // tpk, 0, t % tpk)),
            pl.BlockSpec((1, SLAB, tm), lambda t: (t // tps, 0, t % tps)),
        ],
        out_shape=[
            jax.ShapeDtypeStruct((batch, N_HEADS, seq, SLAB), BF16),
            jax.ShapeDtypeStruct((batch, N_HEADS, seq // KV_TILE, SLAB, KV_TILE), BF16),
            jax.ShapeDtypeStruct((batch, SLAB, seq), F32),
        ],
        scratch_shapes=[pltpu.VMEM((8, SLAB), F32)],
        compiler_params=_cparams(1),
        name="shared_kv",
    )(x, g.reshape(1, d), w_k, w_vT, w_f3, b_f3, k_g2, sel_k, tri)


def _q_kernel(shift_ref, x_ref, mod_ref, g_ref, qwT_ref, qg_ref, fT_ref, selq_ref, o_ref):
    tm, d = x_ref.shape
    h = _norm_mod(x_ref[...], g_ref[...], mod_ref[0], mod_ref[1]).astype(BF16)
    qT = lax.dot_general(qwT_ref[...], h, NT_DIMS, preferred_element_type=F32)
    row = lax.broadcasted_iota(jnp.int32, (SLAB, tm), 0)
    extras = jnp.dot(selq_ref[...], _pieces(fT_ref[0] - shift_ref[0], row),
                     preferred_element_type=F32)
    for hd in range(N_HEADS):
        rows = slice(hd * HEAD_DIM, (hd + 1) * HEAD_DIM)
        qh = qT[rows, :]
        ms = jnp.mean(qh * qh, axis=0, keepdims=True)
        qn = qh * lax.rsqrt(ms + EPS) * qg_ref[...]
        data = slice(0, HEAD_DIM) if hd % 2 == 0 else slice(HEAD_DIM, SLAB)
        other = slice(HEAD_DIM, SLAB) if hd % 2 == 0 else slice(0, HEAD_DIM)
        o_ref[0, hd, data, :] = qn.astype(BF16)
        o_ref[0, hd, other, :] = extras[rows, :].astype(BF16)


def _q_call(shift, x, mod_l, g, q_wT, q_g_cols, fT, sel_q, batch, seq):
    n, d = x.shape
    tm = TOKEN_TILE
    tps = seq // tm
    return pl.pallas_call(
        _q_kernel,
        grid=(n // tm,),
        in_specs=[
            pl.BlockSpec(memory_space=pltpu.SMEM),
            pl.BlockSpec((tm, d), lambda t: (t, 0)),
            pl.BlockSpec((None, N_MOD, 1, d), lambda t: (t // tps, 0, 0, 0)),
            _resident((1, d), lambda t: (0, 0)),
            _resident((d, d), lambda t: (0, 0)),
            _resident((HEAD_DIM, tm), lambda t: (0, 0)),
            pl.BlockSpec((1, SLAB, tm), lambda t: (t // tps, 0, t % tps)),
            _resident((N_HEADS * HEAD_DIM, SLAB), lambda t: (0, 0)),
        ],
        out_specs=pl.BlockSpec((1, N_HEADS, SLAB, tm), lambda t: (t // tps, 0, 0, t % tps)),
        out_shape=jax.ShapeDtypeStruct((batch, N_HEADS, SLAB, seq), BF16),
        compiler_params=_cparams(1),
        name="fox_query",
    )(shift, x, mod_l, g.reshape(1, d), q_wT, q_g_cols, fT, sel_q)


def _attn_kernel(qT_ref, k_ref, vT_ref, o_ref, acc_ref, m_ref):
    tq = qT_ref.shape[-1]
    tk = vT_ref.shape[-1]
    i = pl.program_id(2)
    n_kv = (i * tq + tq + tk - 1) // tk

    for hh in range(2):
        acc_ref[hh] = jnp.zeros((SLAB, tq), F32)
        m_ref[hh] = jnp.full((8, tq), MASK_VALUE, F32)

    def step(kb, masked):
        start = pl.multiple_of(kb * tk, tk)
        for hh in range(2):
            s = jnp.dot(k_ref[0, hh, pl.ds(start, tk), :], qT_ref[0, hh],
                        preferred_element_type=F32)
            if masked:
                key_pos = kb * tk + lax.broadcasted_iota(jnp.int32, (tk, tq), 0)
                q_pos = i * tq + lax.broadcasted_iota(jnp.int32, (tk, tq), 1)
                s = jnp.where(key_pos <= q_pos, s, MASK_VALUE)
            m_old = m_ref[hh, 0:1, :]
            m_new = jnp.maximum(m_old, jnp.max(s, axis=0, keepdims=True))
            alpha = jnp.exp2(m_old - m_new)
            p = jnp.exp2(s - m_new).astype(BF16)
            pv = jnp.dot(vT_ref[0, hh, kb], p, preferred_element_type=F32)
            acc_ref[hh] = alpha * acc_ref[hh] + pv
            m_ref[hh] = jnp.broadcast_to(m_new, (8, tq))

    def body(kb, carry):
        step(kb, False)
        return carry

    lax.fori_loop(0, n_kv - 1, body, 0)
    step(n_kv - 1, True)

    acc0 = acc_ref[0]
    acc1 = acc_ref[1]
    out0 = (acc0 * (1.0 / acc0[HEAD_DIM:HEAD_DIM + 1, :])).T
    out1 = (acc1 * (1.0 / acc1[0:1, :])).T
    lane = lax.broadcasted_iota(jnp.int32, (tq, SLAB), 1)
    o_ref[0] = jnp.where(lane < HEAD_DIM, out0, out1).astype(BF16)


def _attn_preshifted_kernel(qT_ref, k_ref, vT_ref, o_ref, acc_ref, p_ref):
    tq = qT_ref.shape[-1]
    tk = vT_ref.shape[-1]
    i = pl.program_id(2)
    n_kv = (i * tq + tq + tk - 1) // tk

    def probabilities(kb, masked):
        start = pl.multiple_of(kb * tk, tk)
        for hh in range(2):
            s = jnp.dot(k_ref[0, hh, pl.ds(start, tk), :], qT_ref[0, hh],
                        preferred_element_type=F32)
            if masked:
                key_pos = kb * tk + lax.broadcasted_iota(jnp.int32, (tk, tq), 0)
                q_pos = i * tq + lax.broadcasted_iota(jnp.int32, (tk, tq), 1)
                s = jnp.where(key_pos <= q_pos, s, MASK_VALUE)
            p_ref[hh] = jnp.exp2(s).astype(BF16)

    def accumulate(kb):
        for hh in range(2):
            acc_ref[hh] += jnp.dot(vT_ref[0, hh, kb], p_ref[hh], preferred_element_type=F32)

    for hh in range(2):
        acc_ref[hh] = jnp.zeros((SLAB, tq), F32)
    probabilities(n_kv - 1, True)

    def body(kb, pending):
        accumulate(pending)
        probabilities(kb, False)
        return kb

    accumulate(lax.fori_loop(0, n_kv - 1, body, n_kv - 1))

    acc0 = acc_ref[0]
    acc1 = acc_ref[1]
    out0 = (acc0 * (1.0 / acc0[HEAD_DIM:HEAD_DIM + 1, :])).T
    out1 = (acc1 * (1.0 / acc1[0:1, :])).T
    lane = lax.broadcasted_iota(jnp.int32, (tq, SLAB), 1)
    o_ref[0] = jnp.where(lane < HEAD_DIM, out0, out1).astype(BF16)


def _attn_call(qT, k, vT, batch, seq, preshifted):
    nkb = seq // KV_TILE
    pairs = N_HEADS // 2
    if preshifted:
        body = _attn_preshifted_kernel
        scratch = [pltpu.VMEM((2, SLAB, Q_TILE), F32), pltpu.VMEM((2, KV_TILE, Q_TILE), BF16)]
    else:
        body = _attn_kernel
        scratch = [pltpu.VMEM((2, SLAB, Q_TILE), F32), pltpu.VMEM((2, 8, Q_TILE), F32)]
    return pl.pallas_call(
        body,
        grid=(batch, pairs, seq // Q_TILE),
        in_specs=[
            pl.BlockSpec((1, 2, SLAB, Q_TILE), lambda b, j, i: (b, j, 0, i)),
            pl.BlockSpec((1, 2, seq, SLAB), lambda b, j, i: (b, j, 0, 0)),
            pl.BlockSpec((1, 2, nkb, SLAB, KV_TILE), lambda b, j, i: (b, j, 0, 0, 0)),
        ],
        out_specs=pl.BlockSpec((1, Q_TILE, SLAB), lambda b, j, i: (b, i, j)),
        out_shape=jax.ShapeDtypeStruct((batch, seq, N_HEADS * HEAD_DIM), BF16),
        scratch_shapes=scratch,
        compiler_params=_cparams(3),
        name="fox_attention_preshifted" if preshifted else "fox_attention",
    )(qT, k, vT)


def kernel(x, c, ada_w, ada_b, norm_mix_g, norm_mlp_g, sc_w_in, sc_conv, sc_w_out,
           kv_norm_g, w_kv, k_norm_g, w_f, b_f, q_w, q_norm_g, o_w, mlp_w1, mlp_w2):
    batch, seq, d = x.shape
    depth = ada_w.shape[0]
    n_conv = sc_w_in.shape[0]
    assert d == N_HEADS * HEAD_DIM and seq % TOKEN_TILE == 0 and KV_TILE % TOKEN_TILE == 0 and seq % KV_TILE == 0

    mod = _ada_call(c, ada_w, ada_b)
    sel_k, sel_q = _selection_matrices()
    q_scale = LOG2E / np.sqrt(HEAD_DIM)

    xs = x.reshape(batch * seq, d)
    k = vT = fT = None
    for l in range(depth):
        if l < n_conv:
            xs = _conv_call(xs, mod[l], norm_mix_g[l], sc_w_in[l].astype(BF16), sc_conv[l],
                            sc_w_out[l].astype(BF16), seq)
            xs = _mlp_call(xs, mod[l], norm_mlp_g[l], mlp_w1[l].astype(BF16),
                           mlp_w2[l].astype(BF16), seq)
        else:
            i = l - n_conv
            q_g_cols = jnp.broadcast_to((q_norm_g[i] * q_scale)[:, None], (HEAD_DIM, TOKEN_TILE))
            bound = (np.sqrt(HEAD_DIM) * BOUND_SLACK) * jnp.max(jnp.abs(q_norm_g[i])) \
                * jnp.max(jnp.abs(k_norm_g))
            preshift_ok = bound <= MAX_PRESHIFT
            shift = jnp.where(preshift_ok, bound * LOG2E, 0.0).reshape(1).astype(F32)
            qT = _q_call(shift, xs, mod[l], norm_mix_g[l], q_w[i].T.astype(BF16), q_g_cols, fT,
                         sel_q, batch, seq)
            attn = lax.cond(
                preshift_ok,
                functools.partial(_attn_call, batch=batch, seq=seq, preshifted=True),
                functools.partial(_attn_call, batch=batch, seq=seq, preshifted=False),
                qT, k, vT).reshape(batch * seq, d)
            xs = _mlp_call(xs, mod[l], norm_mlp_g[l], mlp_w1[l].astype(BF16),
                           mlp_w2[l].astype(BF16), seq, attn=attn, o_w=o_w[i].astype(BF16))
        if l == n_conv - 1:
            w_f3 = jnp.zeros((d, SLAB), F32).at[:, :3 * N_HEADS].set(jnp.tile(w_f, (1, 3)))
            b_f3 = jnp.zeros((1, SLAB), F32).at[0, :3 * N_HEADS].set(jnp.tile(b_f, 3))
            k, vT, fT = _kv_call(xs, kv_norm_g, w_kv[:, :d].astype(BF16),
                                 w_kv[:, d:].T.astype(BF16), w_f3.astype(BF16), b_f3,
                                 jnp.tile(k_norm_g, 2).reshape(1, SLAB), sel_k, batch, seq)
    return xs.reshape(batch, seq, d)
```

```python
import functools

import numpy as np
import jax
import jax.numpy as jnp
from jax import lax
from jax.experimental import pallas as pl
from jax.experimental.pallas import tpu as pltpu

F32 = jnp.float32
BF16 = jnp.bfloat16

N_HEADS = 16
HEAD_DIM = 64
SLAB = 2 * HEAD_DIM
N_MOD = 6
CONV_W = 3
EPS = 1e-6
MASK_VALUE = -1e30

TOKEN_TILE = 512
FF_CHUNK = 1024
Q_TILE = 1024
KV_TILE = 1024
ADA_COLS = 1536
LOG2E = 1.4426950408889634
MAX_PRESHIFT = 40.0
BOUND_SLACK = 1.0 + 2.0 ** -6
VMEM_LIMIT_BYTES = 56 * 1024 * 1024

NT_DIMS = (((1,), (1,)), ((), ()))


def _cparams(n_axes):
    return pltpu.CompilerParams(
        dimension_semantics=("arbitrary",) * n_axes,
        vmem_limit_bytes=VMEM_LIMIT_BYTES)


def _resident(block_shape, index_map):
    return pl.BlockSpec(block_shape, index_map, pipeline_mode=pl.Buffered(1))


def _rms_scale(x):
    ms = jnp.mean(x * x, axis=-1, keepdims=True)
    return x * lax.rsqrt(ms + EPS)


def _norm_mod(x, g, shift, scale):
    return (_rms_scale(x) * g) * (1.0 + scale) + shift


def _split3(v):
    hi = v.astype(BF16).astype(F32)
    r = v - hi
    mid = r.astype(BF16).astype(F32)
    lo = (r - mid).astype(BF16).astype(F32)
    return hi, mid, lo


def _ada_kernel(c_ref, w_ref, b_ref, o_ref):
    c = c_ref[...]
    ca = c * jax.nn.sigmoid(c)
    o_ref[0] = jnp.dot(ca.astype(BF16), w_ref[0].astype(BF16),
                       preferred_element_type=F32) + b_ref[0]


def _ada_call(c, ada_w, ada_b):
    depth, d, nm = ada_w.shape
    b = c.shape[0]
    rows = 8
    c_pad = jnp.zeros((rows, d), F32).at[:b].set(c)
    out = pl.pallas_call(
        _ada_kernel,
        grid=(depth, nm // ADA_COLS),
        in_specs=[
            pl.BlockSpec((rows, d), lambda l, n: (0, 0)),
            pl.BlockSpec((1, d, ADA_COLS), lambda l, n: (l, 0, n)),
            pl.BlockSpec((1, 1, ADA_COLS), lambda l, n: (l, 0, n)),
        ],
        out_specs=pl.BlockSpec((1, rows, ADA_COLS), lambda l, n: (l, 0, n)),
        out_shape=jax.ShapeDtypeStruct((depth, rows, nm), F32),
        compiler_params=_cparams(2),
        name="ada_mod",
    )(c_pad, ada_w, ada_b.reshape(depth, 1, nm))
    return out[:, :b, :].reshape(depth, b, N_MOD, 1, d)


def _conv_kernel(tiles_per_seq, x_ref, mod_ref, g_ref, win_ref, cw_ref, wout_ref,
                 o_ref, zbuf):
    tm, d = x_ref.shape
    t = pl.program_id(0)

    @pl.when(t % tiles_per_seq == 0)
    def _():
        zbuf[0:8, :] = jnp.zeros((8, d), F32)

    x = x_ref[...]
    h = _norm_mod(x, g_ref[...], mod_ref[0], mod_ref[1]).astype(BF16)
    u = jnp.dot(h, win_ref[...], preferred_element_type=F32)
    z = u[:, d:2 * d] * u[:, 2 * d:]
    zbuf[8:8 + tm, :] = z
    z1 = zbuf[7:7 + tm, :]
    z2 = zbuf[6:6 + tm, :]
    zc = cw_ref[0] * z2 + cw_ref[1] * z1 + cw_ref[2] * z
    zbuf[0:8, :] = z[tm - 8:, :]
    gated = (u[:, :d] * zc).astype(BF16)
    y = jnp.dot(gated, wout_ref[...], preferred_element_type=F32)
    o_ref[...] = x + mod_ref[2] * y


def _conv_call(x, mod_l, g, w_in, conv_w, w_out, seq):
    n, d = x.shape
    tm = TOKEN_TILE
    tps = seq // tm
    return pl.pallas_call(
        functools.partial(_conv_kernel, tps),
        grid=(n // tm,),
        in_specs=[
            pl.BlockSpec((tm, d), lambda t: (t, 0)),
            pl.BlockSpec((None, N_MOD, 1, d), lambda t: (t // tps, 0, 0, 0)),
            _resident((1, d), lambda t: (0, 0)),
            _resident((d, 3 * d), lambda t: (0, 0)),
            _resident((CONV_W, 1, d), lambda t: (0, 0, 0)),
            _resident((d, d), lambda t: (0, 0)),
        ],
        out_specs=pl.BlockSpec((tm, d), lambda t: (t, 0)),
        out_shape=jax.ShapeDtypeStruct((n, d), F32),
        scratch_shapes=[pltpu.VMEM((tm + 8, d), F32)],
        compiler_params=_cparams(1),
        name="conv_mixer",
    )(x, mod_l, g.reshape(1, d), w_in, conv_w.reshape(CONV_W, 1, d), w_out)


def _mlp_residual(x1, mod_ref, g, w1_ref, w2_ref):
    d, ff = w1_ref.shape
    h = _norm_mod(x1, g, mod_ref[3], mod_ref[4]).astype(BF16)
    acc = jnp.zeros(x1.shape, F32)
    for c in range(ff // FF_CHUNK):
        cols = slice(c * FF_CHUNK, (c + 1) * FF_CHUNK)
        a = jnp.maximum(jnp.dot(h, w1_ref[:, cols], preferred_element_type=F32), 0.0)
        acc = acc + jnp.dot((a * a).astype(BF16), w2_ref[cols, :],
                            preferred_element_type=F32)
    return x1 + mod_ref[5] * acc


def _mlp_kernel(x_ref, mod_ref, g_ref, w1_ref, w2_ref, o_ref):
    o_ref[...] = _mlp_residual(x_ref[...], mod_ref, g_ref[...], w1_ref, w2_ref)


def _omlp_kernel(x_ref, a_ref, mod_ref, g_ref, ow_ref, w1_ref, w2_ref, o_ref):
    y = jnp.dot(a_ref[...], ow_ref[...], preferred_element_type=F32)
    x1 = x_ref[...] + mod_ref[2] * y
    o_ref[...] = _mlp_residual(x1, mod_ref, g_ref[...], w1_ref, w2_ref)


def _mlp_call(x, mod_l, g, w1, w2, seq, attn=None, o_w=None):
    n, d = x.shape
    ff = w1.shape[1]
    tm = TOKEN_TILE
    tps = seq // tm
    tile = pl.BlockSpec((tm, d), lambda t: (t, 0))
    mod_spec = pl.BlockSpec((None, N_MOD, 1, d), lambda t: (t // tps, 0, 0, 0))
    g_spec = _resident((1, d), lambda t: (0, 0))
    w_specs = [_resident((d, ff), lambda t: (0, 0)), _resident((ff, d), lambda t: (0, 0))]
    if attn is None:
        kernel, name = _mlp_kernel, "mlp"
        in_specs = [tile, mod_spec, g_spec] + w_specs
        args = (x, mod_l, g.reshape(1, d), w1, w2)
    else:
        kernel, name = _omlp_kernel, "oproj_mlp"
        in_specs = [tile, tile, mod_spec, g_spec, _resident((d, d), lambda t: (0, 0))] + w_specs
        args = (x, attn, mod_l, g.reshape(1, d), o_w, w1, w2)
    return pl.pallas_call(
        kernel,
        grid=(n // tm,),
        in_specs=in_specs,
        out_specs=tile,
        out_shape=jax.ShapeDtypeStruct((n, d), F32),
        compiler_params=_cparams(1),
        name=name,
    )(*args)


def _extra_base(head):
    return HEAD_DIM if head % 2 == 0 else 0


ONE_LANE = SLAB - 1


def _selection_matrices():
    sel_k = np.zeros((SLAB, N_HEADS * SLAB), np.float32)
    sel_q = np.zeros((N_HEADS * HEAD_DIM, SLAB), np.float32)
    for h in range(N_HEADS):
        base = h * SLAB + _extra_base(h)
        for piece in range(3):
            sel_q[h * HEAD_DIM + piece, piece * N_HEADS + h] = 1.0
            sel_q[h * HEAD_DIM + 3 + piece, ONE_LANE] = 1.0
            sel_k[ONE_LANE, base + piece] = 1.0
            sel_k[piece * N_HEADS + h, base + 3 + piece] = -1.0
    return jnp.asarray(sel_k, BF16), jnp.asarray(sel_q, BF16)


def _pieces(f, index):
    hi, mid, lo = _split3(f)
    one = jnp.where(index == ONE_LANE, 1.0, 0.0)
    return jnp.where(index < N_HEADS, hi,
                     jnp.where(index < 2 * N_HEADS, mid,
                               jnp.where(index < 3 * N_HEADS, lo, one))).astype(BF16)


def _kv_kernel(tiles_per_seq, x_ref, g_ref, wk_ref, wvT_ref, wf_ref, bf_ref, kg_ref,
               selk_ref, tri_ref, k_ref, vT_ref, fT_ref, carry):
    tm, d = x_ref.shape
    t = pl.program_id(0)

    @pl.when(t % tiles_per_seq == 0)
    def _():
        carry[...] = jnp.zeros(carry.shape, F32)

    h = (_rms_scale(x_ref[...]) * g_ref[...]).astype(BF16)

    xf = jnp.dot(h, wf_ref[...], preferred_element_type=F32) + bf_ref[...]
    log_f = jnp.minimum(xf, 0.0) - jnp.log1p(jnp.exp(-jnp.abs(xf)))
    tri = tri_ref[...]
    f = carry[0:1, :]
    for piece in _split3(log_f):
        f = f + jnp.dot(tri, piece.astype(BF16), preferred_element_type=F32)
    carry[...] = jnp.broadcast_to(f[tm - 1:tm, :], carry.shape)
    f2 = f * LOG2E
    fT_ref[0] = f2.T

    lane = lax.broadcasted_iota(jnp.int32, (tm, SLAB), 1)
    extras = jnp.dot(_pieces(f2, lane), selk_ref[...], preferred_element_type=F32)

    k = jnp.dot(h, wk_ref[...], preferred_element_type=F32)
    low = lane < HEAD_DIM
    for j in range(N_HEADS // 2):
        ks = k[:, j * SLAB:(j + 1) * SLAB]
        k2 = ks * ks
        ms_lo = jnp.sum(jnp.where(low, k2, 0.0), axis=-1, keepdims=True) * (1.0 / HEAD_DIM)
        ms_hi = jnp.sum(jnp.where(low, 0.0, k2), axis=-1, keepdims=True) * (1.0 / HEAD_DIM)
        kn = ks * jnp.where(low, lax.rsqrt(ms_lo + EPS), lax.rsqrt(ms_hi + EPS)) * kg_ref[...]
        ex_even = extras[:, (2 * j) * SLAB:(2 * j + 1) * SLAB]
        ex_odd = extras[:, (2 * j + 1) * SLAB:(2 * j + 2) * SLAB]
        k_ref[0, 2 * j] = jnp.where(low, kn, ex_even).astype(BF16)
        k_ref[0, 2 * j + 1] = jnp.where(low, ex_odd, kn).astype(BF16)

    vT = lax.dot_general(wvT_ref[...], h, NT_DIMS, preferred_element_type=F32)
    row = lax.broadcasted_iota(jnp.int32, (HEAD_DIM, tm), 0)
    ones_blk = jnp.where(row == 0, 1.0, 0.0).astype(BF16)
    for hd in range(N_HEADS):
        data = slice(0, HEAD_DIM) if hd % 2 == 0 else slice(HEAD_DIM, SLAB)
        other = slice(HEAD_DIM, SLAB) if hd % 2 == 0 else slice(0, HEAD_DIM)
        vT_ref[0, hd, 0, data, :] = vT[hd * HEAD_DIM:(hd + 1) * HEAD_DIM, :].astype(BF16)
        vT_ref[0, hd, 0, other, :] = ones_blk


def _kv_call(x, g, w_k, w_vT, w_f3, b_f3, k_g2, sel_k, batch, seq):
    n, d = x.shape
    tm = TOKEN_TILE
    tps = seq // tm
    tpk = KV_TILE // tm
    tri = jnp.asarray(np.tril(np.ones((tm, tm), np.float32)), BF16)
    return pl.pallas_call(
        functools.partial(_kv_kernel, tps),
        grid=(n // tm,),
        in_specs=[
            pl.BlockSpec((tm, d), lambda t: (t, 0)),
            _resident((1, d), lambda t: (0, 0)),
            _resident((d, d), lambda t: (0, 0)),
            _resident((d, d), lambda t: (0, 0)),
            _resident((d, SLAB), lambda t: (0, 0)),
            _resident((1, SLAB), lambda t: (0, 0)),
            _resident((1, SLAB), lambda t: (0, 0)),
            _resident((SLAB, N_HEADS * SLAB), lambda t: (0, 0)),
            _resident((tm, tm), lambda t: (0, 0)),
        ],
        out_specs=[
            pl.BlockSpec((1, N_HEADS, tm, SLAB), lambda t: (t // tps, 0, t % tps, 0)),
            pl.BlockSpec((1, N_HEADS, 1, SLAB, tm),
                         lambda t: (t // tps, 0, (t % tps) // tpk, 0, t % tpk)),
            pl.BlockSpec((1, SLAB, tm), lambda t: (t // tps, 0, t % tps)),
        ],
        out_shape=[
            jax.ShapeDtypeStruct((batch, N_HEADS, seq, SLAB), BF16),
            jax.ShapeDtypeStruct((batch, N_HEADS, seq // KV_TILE, SLAB, KV_TILE), BF16),
            jax.ShapeDtypeStruct((batch, SLAB, seq), F32),
        ],
        scratch_shapes=[pltpu.VMEM((8, SLAB), F32)],
        compiler_params=_cparams(1),
        name="shared_kv",
    )(x, g.reshape(1, d), w_k, w_vT, w_f3, b_f3, k_g2, sel_k, tri)


def _q_kernel(shift_ref, x_ref, mod_ref, g_ref, qwT_ref, qg_ref, fT_ref, selq_ref, o_ref):
    tm, d = x_ref.shape
    h = _norm_mod(x_ref[...], g_ref[...], mod_ref[0], mod_ref[1]).astype(BF16)
    qT = lax.dot_general(qwT_ref[...], h, NT_DIMS, preferred_element_type=F32)
    row = lax.broadcasted_iota(jnp.int32, (SLAB, tm), 0)
    extras = jnp.dot(selq_ref[...], _pieces(fT_ref[0] - shift_ref[0], row),
                     preferred_element_type=F32)
    for hd in range(N_HEADS):
        rows = slice(hd * HEAD_DIM, (hd + 1) * HEAD_DIM)
        qh = qT[rows, :]
        ms = jnp.mean(qh * qh, axis=0, keepdims=True)
        qn = qh * lax.rsqrt(ms + EPS) * qg_ref[...]
        data = slice(0, HEAD_DIM) if hd % 2 == 0 else slice(HEAD_DIM, SLAB)
        other = slice(HEAD_DIM, SLAB) if hd % 2 == 0 else slice(0, HEAD_DIM)
        o_ref[0, hd, data, :] = qn.astype(BF16)
        o_ref[0, hd, other, :] = extras[rows, :].astype(BF16)


def _q_call(shift, x, mod_l, g, q_wT, q_g_cols, fT, sel_q, batch, seq):
    n, d = x.shape
    tm = TOKEN_TILE
    tps = seq // tm
    return pl.pallas_call(
        _q_kernel,
        grid=(n // tm,),
        in_specs=[
            pl.BlockSpec(memory_space=pltpu.SMEM),
            pl.BlockSpec((tm, d), lambda t: (t, 0)),
            pl.BlockSpec((None, N_MOD, 1, d), lambda t: (t // tps, 0, 0, 0)),
            _resident((1, d), lambda t: (0, 0)),
            _resident((d, d), lambda t: (0, 0)),
            _resident((HEAD_DIM, tm), lambda t: (0, 0)),
            pl.BlockSpec((1, SLAB, tm), lambda t: (t // tps, 0, t % tps)),
            _resident((N_HEADS * HEAD_DIM, SLAB), lambda t: (0, 0)),
        ],
        out_specs=pl.BlockSpec((1, N_HEADS, SLAB, tm), lambda t: (t // tps, 0, 0, t % tps)),
        out_shape=jax.ShapeDtypeStruct((batch, N_HEADS, SLAB, seq), BF16),
        compiler_params=_cparams(1),
        name="fox_query",
    )(shift, x, mod_l, g.reshape(1, d), q_wT, q_g_cols, fT, sel_q)


def _attn_kernel(qT_ref, k_ref, vT_ref, o_ref, acc_ref, m_ref):
    tq = qT_ref.shape[-1]
    tk = vT_ref.shape[-1]
    i = pl.program_id(2)
    n_kv = (i * tq + tq + tk - 1) // tk

    for hh in range(2):
        acc_ref[hh] = jnp.zeros((SLAB, tq), F32)
        m_ref[hh] = jnp.full((8, tq), MASK_VALUE, F32)

    def step(kb, masked):
        start = pl.multiple_of(kb * tk, tk)
        for hh in range(2):
            s = jnp.dot(k_ref[0, hh, pl.ds(start, tk), :], qT_ref[0, hh],
                        preferred_element_type=F32)
            if masked:
                key_pos = kb * tk + lax.broadcasted_iota(jnp.int32, (tk, tq), 0)
                q_pos = i * tq + lax.broadcasted_iota(jnp.int32, (tk, tq), 1)
                s = jnp.where(key_pos <= q_pos, s, MASK_VALUE)
            m_old = m_ref[hh, 0:1, :]
            m_new = jnp.maximum(m_old, jnp.max(s, axis=0, keepdims=True))
            alpha = jnp.exp2(m_old - m_new)
            p = jnp.exp2(s - m_new).astype(BF16)
            pv = jnp.dot(vT_ref[0, hh, kb], p, preferred_element_type=F32)
            acc_ref[hh] = alpha * acc_ref[hh] + pv
            m_ref[hh] = jnp.broadcast_to(m_new, (8, tq))

    def body(kb, carry):
        step(kb, False)
        return carry

    lax.fori_loop(0, n_kv - 1, body, 0)
    step(n_kv - 1, True)

    acc0 = acc_ref[0]
    acc1 = acc_ref[1]
    out0 = (acc0 * (1.0 / acc0[HEAD_DIM:HEAD_DIM + 1, :])).T
    out1 = (acc1 * (1.0 / acc1[0:1, :])).T
    lane = lax.broadcasted_iota(jnp.int32, (tq, SLAB), 1)
    o_ref[0] = jnp.where(lane < HEAD_DIM, out0, out1).astype(BF16)


def _attn_preshifted_kernel(qT_ref, k_ref, vT_ref, o_ref, acc_ref, p_ref):
    tq = qT_ref.shape[-1]
    tk = vT_ref.shape[-1]
    i = pl.program_id(2)
    n_kv = (i * tq + tq + tk - 1) // tk

    def probabilities(kb, masked):
        start = pl.multiple_of(kb * tk, tk)
        for hh in range(2):
            s = jnp.dot(k_ref[0, hh, pl.ds(start, tk), :], qT_ref[0, hh],
                        preferred_element_type=F32)
            if masked:
                key_pos = kb * tk + lax.broadcasted_iota(jnp.int32, (tk, tq), 0)
                q_pos = i * tq + lax.broadcasted_iota(jnp.int32, (tk, tq), 1)
                s = jnp.where(key_pos <= q_pos, s, MASK_VALUE)
            p_ref[hh] = jnp.exp2(s).astype(BF16)

    def accumulate(kb):
        for hh in range(2):
            acc_ref[hh] += jnp.dot(vT_ref[0, hh, kb], p_ref[hh], preferred_element_type=F32)

    for hh in range(2):
        acc_ref[hh] = jnp.zeros((SLAB, tq), F32)
    probabilities(n_kv - 1, True)

    def body(kb, pending):
        accumulate(pending)
        probabilities(kb, False)
        return kb

    accumulate(lax.fori_loop(0, n_kv - 1, body, n_kv - 1))

    acc0 = acc_ref[0]
    acc1 = acc_ref[1]
    out0 = (acc0 * (1.0 / acc0[HEAD_DIM:HEAD_DIM + 1, :])).T
    out1 = (acc1 * (1.0 / acc1[0:1, :])).T
    lane = lax.broadcasted_iota(jnp.int32, (tq, SLAB), 1)
    o_ref[0] = jnp.where(lane < HEAD_DIM, out0, out1).astype(BF16)


def _attn_call(qT, k, vT, batch, seq, preshifted):
    nkb = seq // KV_TILE
    pairs = N_HEADS // 2
    if preshifted:
        body = _attn_preshifted_kernel
        scratch = [pltpu.VMEM((2, SLAB, Q_TILE), F32), pltpu.VMEM((2, KV_TILE, Q_TILE), BF16)]
    else:
        body = _attn_kernel
        scratch = [pltpu.VMEM((2, SLAB, Q_TILE), F32), pltpu.VMEM((2, 8, Q_TILE), F32)]
    return pl.pallas_call(
        body,
        grid=(batch, pairs, seq // Q_TILE),
        in_specs=[
            pl.BlockSpec((1, 2, SLAB, Q_TILE), lambda b, j, i: (b, j, 0, i)),
            pl.BlockSpec((1, 2, seq, SLAB), lambda b, j, i: (b, j, 0, 0)),
            pl.BlockSpec((1, 2, nkb, SLAB, KV_TILE), lambda b, j, i: (b, j, 0, 0, 0)),
        ],
        out_specs=pl.BlockSpec((1, Q_TILE, SLAB), lambda b, j, i: (b, i, j)),
        out_shape=jax.ShapeDtypeStruct((batch, seq, N_HEADS * HEAD_DIM), BF16),
        scratch_shapes=scratch,
        compiler_params=_cparams(3),
        name="fox_attention_preshifted" if preshifted else "fox_attention",
    )(qT, k, vT)


def kernel(x, c, ada_w, ada_b, norm_mix_g, norm_mlp_g, sc_w_in, sc_conv, sc_w_out,
           kv_norm_g, w_kv, k_norm_g, w_f, b_f, q_w, q_norm_g, o_w, mlp_w1, mlp_w2):
    batch, seq, d = x.shape
    depth = ada_w.shape[0]
    n_conv = sc_w_in.shape[0]
    assert d == N_HEADS * HEAD_DIM and seq % TOKEN_TILE == 0 and KV_TILE % TOKEN_TILE == 0 and seq % KV_TILE == 0

    mod = _ada_call(c, ada_w, ada_b)
    sel_k, sel_q = _selection_matrices()
    q_scale = LOG2E / np.sqrt(HEAD_DIM)

    xs = x.reshape(batch * seq, d)
    k = vT = fT = None
    for l in range(depth):
        if l < n_conv:
            xs = _conv_call(xs, mod[l], norm_mix_g[l], sc_w_in[l].astype(BF16), sc_conv[l],
                            sc_w_out[l].astype(BF16), seq)
            xs = _mlp_call(xs, mod[l], norm_mlp_g[l], mlp_w1[l].astype(BF16),
                           mlp_w2[l].astype(BF16), seq)
        else:
            i = l - n_conv
            q_g_cols = jnp.broadcast_to((q_norm_g[i] * q_scale)[:, None], (HEAD_DIM, TOKEN_TILE))
            bound = (np.sqrt(HEAD_DIM) * BOUND_SLACK) * jnp.max(jnp.abs(q_norm_g[i])) \
                * jnp.max(jnp.abs(k_norm_g))
            preshift_ok = bound <= MAX_PRESHIFT
            shift = jnp.where(preshift_ok, bound * LOG2E, 0.0).reshape(1).astype(F32)
            qT = _q_call(shift, xs, mod[l], norm_mix_g[l], q_w[i].T.astype(BF16), q_g_cols, fT,
                         sel_q, batch, seq)
            attn = lax.cond(
                preshift_ok,
                functools.partial(_attn_call, batch=batch, seq=seq, preshifted=True),
                functools.partial(_attn_call, batch=batch, seq=seq, preshifted=False),
                qT, k, vT).reshape(batch * seq, d)
            xs = _mlp_call(xs, mod[l], norm_mlp_g[l], mlp_w1[l].astype(BF16),
                           mlp_w2[l].astype(BF16), seq, attn=attn, o_w=o_w[i].astype(BF16))
        if l == n_conv - 1:
            w_f3 = jnp.zeros((d, SLAB), F32).at[:, :3 * N_HEADS].set(jnp.tile(w_f, (1, 3)))
            b_f3 = jnp.zeros((1, SLAB), F32).at[0, :3 * N_HEADS].set(jnp.tile(b_f, 3))
            k, vT, fT = _kv_call(xs, kv_norm_g, w_kv[:, :d].astype(BF16),
                                 w_kv[:, d:].T.astype(BF16), w_f3.astype(BF16), b_f3,
                                 jnp.tile(k_norm_g, 2).reshape(1, SLAB), sel_k, batch, seq)
    return xs.reshape(batch, seq, d)
```

```python
import functools

import numpy as np
import jax
import jax.numpy as jnp
from jax import lax
from jax.experimental import pallas as pl
from jax.experimental.pallas import tpu as pltpu

F32 = jnp.float32
BF16 = jnp.bfloat16

N_HEADS = 16
HEAD_DIM = 64
SLAB = 2 * HEAD_DIM
N_MOD = 6
CONV_W = 3
EPS = 1e-6
MASK_VALUE = -1e30

TOKEN_TILE = 512
FF_CHUNK = 1024
CONV_ROW_GROUPS = 1
Q_TILE = 1024
KV_TILE = 1024
ADA_COLS = 1536
LOG2E = 1.4426950408889634
MAX_PRESHIFT = 40.0
BOUND_SLACK = 1.0 + 2.0 ** -6
VMEM_LIMIT_BYTES = 56 * 1024 * 1024

NT_DIMS = (((1,), (1,)), ((), ()))


def _cparams(n_axes):
    return pltpu.CompilerParams(
        dimension_semantics=("arbitrary",) * n_axes,
        vmem_limit_bytes=VMEM_LIMIT_BYTES)


def _resident(block_shape, index_map):
    return pl.BlockSpec(block_shape, index_map, pipeline_mode=pl.Buffered(1))


def _layer_spec(stack, layer):
    _, rows, cols = stack.shape
    return _resident((None, rows, cols), lambda *_: (layer, 0, 0))


def _rms_scale(x):
    ms = jnp.mean(x * x, axis=-1, keepdims=True)
    return x * lax.rsqrt(ms + EPS)


def _norm_mod(x, g, shift, scale):
    return (_rms_scale(x) * g) * (1.0 + scale) + shift


def _split3(v):
    hi = v.astype(BF16).astype(F32)
    r = v - hi
    mid = r.astype(BF16).astype(F32)
    lo = (r - mid).astype(BF16).astype(F32)
    return hi, mid, lo


def _ada_kernel(c_ref, w_ref, b_ref, o_ref):
    c = c_ref[...]
    ca = c * jax.nn.sigmoid(c)
    o_ref[0] = jnp.dot(ca.astype(BF16), w_ref[0].astype(BF16),
                       preferred_element_type=F32) + b_ref[0]


def _ada_call(c, ada_w, ada_b):
    depth, d, nm = ada_w.shape
    b = c.shape[0]
    rows = 8
    c_pad = jnp.zeros((rows, d), F32).at[:b].set(c)
    out = pl.pallas_call(
        _ada_kernel,
        grid=(depth, nm // ADA_COLS),
        in_specs=[
            pl.BlockSpec((rows, d), lambda l, n: (0, 0)),
            pl.BlockSpec((1, d, ADA_COLS), lambda l, n: (l, 0, n)),
            pl.BlockSpec((1, 1, ADA_COLS), lambda l, n: (l, 0, n)),
        ],
        out_specs=pl.BlockSpec((1, rows, ADA_COLS), lambda l, n: (l, 0, n)),
        out_shape=jax.ShapeDtypeStruct((depth, rows, nm), F32),
        compiler_params=_cparams(2),
        name="ada_mod",
    )(c_pad, ada_w, ada_b.reshape(depth, 1, nm))
    return out[:, :b, :].reshape(depth, b, N_MOD, 1, d)


def _conv_kernel(tiles_per_seq, x_ref, mod_ref, g_ref, win_ref, cw_ref, wout_ref,
                 o_ref, zbuf):
    tm, d = x_ref.shape
    t = pl.program_id(0)

    @pl.when(t % tiles_per_seq == 0)
    def _():
        zbuf[0:8, :] = jnp.zeros((8, d), F32)

    rg = tm // CONV_ROW_GROUPS
    for r in range(CONV_ROW_GROUPS):
        rows = slice(r * rg, (r + 1) * rg)
        x = x_ref[rows, :]
        h = _norm_mod(x, g_ref[...], mod_ref[0], mod_ref[1]).astype(BF16)
        u = jnp.dot(h, win_ref[...], preferred_element_type=F32)
        z = u[:, d:2 * d] * u[:, 2 * d:]
        zbuf[8 + r * rg:8 + (r + 1) * rg, :] = z
        z1 = zbuf[7 + r * rg:7 + (r + 1) * rg, :]
        z2 = zbuf[6 + r * rg:6 + (r + 1) * rg, :]
        zc = cw_ref[0] * z2 + cw_ref[1] * z1 + cw_ref[2] * z
        gated = (u[:, :d] * zc).astype(BF16)
        y = jnp.dot(gated, wout_ref[...], preferred_element_type=F32)
        o_ref[rows, :] = x + mod_ref[2] * y
    zbuf[0:8, :] = zbuf[tm:tm + 8, :]


def _conv_call(x, mod_l, g, w_in, conv_w, w_out, layer, seq):
    n, d = x.shape
    tm = TOKEN_TILE
    tps = seq // tm
    return pl.pallas_call(
        functools.partial(_conv_kernel, tps),
        grid=(n // tm,),
        in_specs=[
            pl.BlockSpec((tm, d), lambda t: (t, 0)),
            pl.BlockSpec((None, N_MOD, 1, d), lambda t: (t // tps, 0, 0, 0)),
            _resident((1, d), lambda t: (0, 0)),
            _layer_spec(w_in, layer),
            _resident((CONV_W, 1, d), lambda t: (0, 0, 0)),
            _layer_spec(w_out, layer),
        ],
        out_specs=pl.BlockSpec((tm, d), lambda t: (t, 0)),
        out_shape=jax.ShapeDtypeStruct((n, d), F32),
        scratch_shapes=[pltpu.VMEM((tm + 8, d), F32)],
        compiler_params=_cparams(1),
        name="conv_mixer",
    )(x, mod_l, g.reshape(1, d), w_in, conv_w.reshape(CONV_W, 1, d), w_out)


def _mlp_residual(x1, mod_ref, g, w1_ref, w2_ref):
    d, ff = w1_ref.shape
    h = _norm_mod(x1, g, mod_ref[3], mod_ref[4]).astype(BF16)
    acc = jnp.zeros(x1.shape, F32)
    for c in range(ff // FF_CHUNK):
        cols = slice(c * FF_CHUNK, (c + 1) * FF_CHUNK)
        a = jnp.maximum(jnp.dot(h, w1_ref[:, cols], preferred_element_type=F32), 0.0)
        acc = acc + jnp.dot((a * a).astype(BF16), w2_ref[cols, :],
                            preferred_element_type=F32)
    return x1 + mod_ref[5] * acc


def _mlp_kernel(x_ref, mod_ref, g_ref, w1_ref, w2_ref, o_ref):
    o_ref[...] = _mlp_residual(x_ref[...], mod_ref, g_ref[...], w1_ref, w2_ref)


def _omlp_kernel(x_ref, a_ref, mod_ref, g_ref, ow_ref, w1_ref, w2_ref, o_ref):
    y = jnp.dot(a_ref[...], ow_ref[...], preferred_element_type=F32)
    x1 = x_ref[...] + mod_ref[2] * y
    o_ref[...] = _mlp_residual(x1, mod_ref, g_ref[...], w1_ref, w2_ref)


def _mlp_call(x, mod_l, g, w1, w2, layer, seq, attn=None, o_w=None, o_layer=None):
    n, d = x.shape
    tm = TOKEN_TILE
    tps = seq // tm
    tile = pl.BlockSpec((tm, d), lambda t: (t, 0))
    mod_spec = pl.BlockSpec((None, N_MOD, 1, d), lambda t: (t // tps, 0, 0, 0))
    g_spec = _resident((1, d), lambda t: (0, 0))
    w_specs = [_layer_spec(w1, layer), _layer_spec(w2, layer)]
    if attn is None:
        kernel, name = _mlp_kernel, "mlp"
        in_specs = [tile, mod_spec, g_spec] + w_specs
        args = (x, mod_l, g.reshape(1, d), w1, w2)
    else:
        kernel, name = _omlp_kernel, "oproj_mlp"
        in_specs = [tile, tile, mod_spec, g_spec, _layer_spec(o_w, o_layer)] + w_specs
        args = (x, attn, mod_l, g.reshape(1, d), o_w, w1, w2)
    return pl.pallas_call(
        kernel,
        grid=(n // tm,),
        in_specs=in_specs,
        out_specs=tile,
        out_shape=jax.ShapeDtypeStruct((n, d), F32),
        compiler_params=_cparams(1),
        name=name,
    )(*args)


def _extra_base(head):
    return HEAD_DIM if head % 2 == 0 else 0


ONE_LANE = SLAB - 1


EXTRA_ROWS = 16


def _selection_matrices():
    sel_k = np.zeros((SLAB, N_HEADS // 2 * SLAB), np.float32)
    sel_q = np.zeros((N_HEADS * EXTRA_ROWS, SLAB), np.float32)
    for h in range(N_HEADS):
        base = (h // 2) * SLAB + _extra_base(h)
        for piece in range(3):
            sel_q[h * EXTRA_ROWS + piece, piece * N_HEADS + h] = 1.0
            sel_q[h * EXTRA_ROWS + 3 + piece, ONE_LANE] = 1.0
            sel_k[ONE_LANE, base + piece] = 1.0
            sel_k[piece * N_HEADS + h, base + 3 + piece] = -1.0
    return jnp.asarray(sel_k, BF16), jnp.asarray(sel_q, BF16)


def _pieces(f, index):
    hi, mid, lo = _split3(f)
    one = jnp.where(index == ONE_LANE, 1.0, 0.0)
    return jnp.where(index < N_HEADS, hi,
                     jnp.where(index < 2 * N_HEADS, mid,
                               jnp.where(index < 3 * N_HEADS, lo, one))).astype(BF16)


def _kv_kernel(tiles_per_seq, x_ref, g_ref, wk_ref, wvT_ref, wf_ref, bf_ref, kg_ref,
               selk_ref, tri_ref, k_ref, vT_ref, fT_ref, carry):
    tm, d = x_ref.shape
    t = pl.program_id(0)

    @pl.when(t % tiles_per_seq == 0)
    def _():
        carry[...] = jnp.zeros(carry.shape, F32)

    h = (_rms_scale(x_ref[...]) * g_ref[...]).astype(BF16)

    xf = jnp.dot(h, wf_ref[...], preferred_element_type=F32) + bf_ref[...]
    log_f = jnp.minimum(xf, 0.0) - jnp.log1p(jnp.exp(-jnp.abs(xf)))
    tri = tri_ref[...]
    f = carry[0:1, :]
    for piece in _split3(log_f):
        f = f + jnp.dot(tri, piece.astype(BF16), preferred_element_type=F32)
    carry[...] = jnp.broadcast_to(f[tm - 1:tm, :], carry.shape)
    f2 = f * LOG2E
    fT_ref[0] = f2.T

    lane = lax.broadcasted_iota(jnp.int32, (tm, SLAB), 1)
    extras = jnp.dot(_pieces(f2, lane), selk_ref[...], preferred_element_type=F32)

    k = jnp.dot(h, wk_ref[...], preferred_element_type=F32)
    low = lane < HEAD_DIM
    for j in range(N_HEADS // 2):
        ks = k[:, j * SLAB:(j + 1) * SLAB]
        k2 = ks * ks
        ms_lo = jnp.sum(jnp.where(low, k2, 0.0), axis=-1, keepdims=True) * (1.0 / HEAD_DIM)
        ms_hi = jnp.sum(jnp.where(low, 0.0, k2), axis=-1, keepdims=True) * (1.0 / HEAD_DIM)
        kn = ks * jnp.where(low, lax.rsqrt(ms_lo + EPS), lax.rsqrt(ms_hi + EPS)) * kg_ref[...]
        ex = extras[:, j * SLAB:(j + 1) * SLAB]
        k_ref[0, 2 * j] = jnp.where(low, kn, ex).astype(BF16)
        k_ref[0, 2 * j + 1] = jnp.where(low, ex, kn).astype(BF16)

    vT = lax.dot_general(wvT_ref[...], h, NT_DIMS, preferred_element_type=F32)
    row = lax.broadcasted_iota(jnp.int32, (HEAD_DIM, tm), 0)
    ones_blk = jnp.where(row == 0, 1.0, 0.0).astype(BF16)
    for hd in range(N_HEADS):
        data = slice(0, HEAD_DIM) if hd % 2 == 0 else slice(HEAD_DIM, SLAB)
        other = slice(HEAD_DIM, SLAB) if hd % 2 == 0 else slice(0, HEAD_DIM)
        vT_ref[0, hd, 0, data, :] = vT[hd * HEAD_DIM:(hd + 1) * HEAD_DIM, :].astype(BF16)
        vT_ref[0, hd, 0, other, :] = ones_blk


def _kv_call(x, g, w_k, w_vT, w_f3, b_f3, k_g2, sel_k, batch, seq):
    n, d = x.shape
    tm = TOKEN_TILE
    tps = seq // tm
    tpk = KV_TILE // tm
    tri = jnp.asarray(np.tril(np.ones((tm, tm), np.float32)), BF16)
    return pl.pallas_call(
        functools.partial(_kv_kernel, tps),
        grid=(n // tm,),
        in_specs=[
            pl.BlockSpec((tm, d), lambda t: (t, 0)),
            _resident((1, d), lambda t: (0, 0)),
            _resident((d, d), lambda t: (0, 0)),
            _resident((d, d), lambda t: (0, 0)),
            _resident((d, SLAB), lambda t: (0, 0)),
            _resident((1, SLAB), lambda t: (0, 0)),
            _resident((1, SLAB), lambda t: (0, 0)),
            _resident((SLAB, N_HEADS // 2 * SLAB), lambda t: (0, 0)),
            _resident((tm, tm), lambda t: (0, 0)),
        ],
        out_specs=[
            pl.BlockSpec((1, N_HEADS, tm, SLAB), lambda t: (t // tps, 0, t % tps, 0)),
            pl.BlockSpec((1, N_HEADS, 1, SLAB, tm),
                         lambda t: (t // tps, 0, (t % tps) // tpk, 0, t % tpk)),
            pl.BlockSpec((1, SLAB, tm), lambda t: (t // tps, 0, t % tps)),
        ],
        out_shape=[
            jax.ShapeDtypeStruct((batch, N_HEADS, seq, SLAB), BF16),
            jax.ShapeDtypeStruct((batch, N_HEADS, seq // KV_TILE, SLAB, KV_TILE), BF16),
            jax.ShapeDtypeStruct((batch, SLAB, seq), F32),
        ],
        scratch_shapes=[pltpu.VMEM((8, SLAB), F32)],
        compiler_params=_cparams(1),
        name="shared_kv",
    )(x, g.reshape(1, d), w_k, w_vT, w_f3, b_f3, k_g2, sel_k, tri)


def _q_kernel(shift_ref, x_ref, mod_ref, g_ref, qwT_ref, qg_ref, fT_ref, selq_ref, o_ref):
    tm, d = x_ref.shape
    h = _norm_mod(x_ref[...], g_ref[...], mod_ref[0], mod_ref[1]).astype(BF16)
    qT = lax.dot_general(qwT_ref[...], h, NT_DIMS, preferred_element_type=F32)
    row = lax.broadcasted_iota(jnp.int32, (SLAB, tm), 0)
    extras = jnp.dot(selq_ref[...], _pieces(fT_ref[0] - shift_ref[0], row),
                     preferred_element_type=F32)
    unused = jnp.zeros((HEAD_DIM - EXTRA_ROWS, tm), BF16)
    for hd in range(N_HEADS):
        qh = qT[hd * HEAD_DIM:(hd + 1) * HEAD_DIM, :]
        ms = jnp.mean(qh * qh, axis=0, keepdims=True)
        qn = qh * lax.rsqrt(ms + EPS) * qg_ref[...]
        data = 0 if hd % 2 == 0 else HEAD_DIM
        other = HEAD_DIM - data
        o_ref[0, hd, data:data + HEAD_DIM, :] = qn.astype(BF16)
        o_ref[0, hd, other:other + EXTRA_ROWS, :] = (
            extras[hd * EXTRA_ROWS:(hd + 1) * EXTRA_ROWS, :].astype(BF16))
        o_ref[0, hd, other + EXTRA_ROWS:other + HEAD_DIM, :] = unused


def _q_call(shift, x, mod_l, g, q_wT, layer, q_g_cols, fT, sel_q, batch, seq):
    n, d = x.shape
    tm = TOKEN_TILE
    tps = seq // tm
    return pl.pallas_call(
        _q_kernel,
        grid=(n // tm,),
        in_specs=[
            pl.BlockSpec(memory_space=pltpu.SMEM),
            pl.BlockSpec((tm, d), lambda t: (t, 0)),
            pl.BlockSpec((None, N_MOD, 1, d), lambda t: (t // tps, 0, 0, 0)),
            _resident((1, d), lambda t: (0, 0)),
            _layer_spec(q_wT, layer),
            _resident((HEAD_DIM, tm), lambda t: (0, 0)),
            pl.BlockSpec((1, SLAB, tm), lambda t: (t // tps, 0, t % tps)),
            _resident((N_HEADS * EXTRA_ROWS, SLAB), lambda t: (0, 0)),
        ],
        out_specs=pl.BlockSpec((1, N_HEADS, SLAB, tm), lambda t: (t // tps, 0, 0, t % tps)),
        out_shape=jax.ShapeDtypeStruct((batch, N_HEADS, SLAB, seq), BF16),
        compiler_params=_cparams(1),
        name="fox_query",
    )(shift, x, mod_l, g.reshape(1, d), q_wT, q_g_cols, fT, sel_q)


def _attn_kernel(qT_ref, k_ref, vT_ref, o_ref, acc_ref, m_ref):
    tq = qT_ref.shape[-1]
    tk = vT_ref.shape[-1]
    i = pl.program_id(2)
    n_kv = (i * tq + tq + tk - 1) // tk

    for hh in range(2):
        acc_ref[hh] = jnp.zeros((SLAB, tq), F32)
        m_ref[hh] = jnp.full((8, tq), MASK_VALUE, F32)

    def step(kb, masked):
        start = pl.multiple_of(kb * tk, tk)
        for hh in range(2):
            s = jnp.dot(k_ref[0, hh, pl.ds(start, tk), :], qT_ref[0, hh],
                        preferred_element_type=F32)
            if masked:
                key_pos = kb * tk + lax.broadcasted_iota(jnp.int32, (tk, tq), 0)
                q_pos = i * tq + lax.broadcasted_iota(jnp.int32, (tk, tq), 1)
                s = jnp.where(key_pos <= q_pos, s, MASK_VALUE)
            m_old = m_ref[hh, 0:1, :]
            m_new = jnp.maximum(m_old, jnp.max(s, axis=0, keepdims=True))
            alpha = jnp.exp2(m_old - m_new)
            p = jnp.exp2(s - m_new).astype(BF16)
            pv = jnp.dot(vT_ref[0, hh, kb], p, preferred_element_type=F32)
            acc_ref[hh] = alpha * acc_ref[hh] + pv
            m_ref[hh] = jnp.broadcast_to(m_new, (8, tq))

    def body(kb, carry):
        step(kb, False)
        return carry

    lax.fori_loop(0, n_kv - 1, body, 0)
    step(n_kv - 1, True)

    acc0 = acc_ref[0]
    acc1 = acc_ref[1]
    out0 = (acc0 * (1.0 / acc0[HEAD_DIM:HEAD_DIM + 1, :])).T
    out1 = (acc1 * (1.0 / acc1[0:1, :])).T
    lane = lax.broadcasted_iota(jnp.int32, (tq, SLAB), 1)
    o_ref[0] = jnp.where(lane < HEAD_DIM, out0, out1).astype(BF16)


def _attn_preshifted_kernel(qT_ref, k_ref, vT_ref, o_ref, acc_ref, p_ref):
    tq = qT_ref.shape[-1]
    tk = vT_ref.shape[-1]
    assert tq == tk
    half = tk // 2
    i = pl.program_id(2)

    def probabilities(kb, slot):
        start = pl.multiple_of(kb * tk, tk)
        for hh in range(2):
            s = jnp.dot(k_ref[0, hh, pl.ds(start, tk), :], qT_ref[0, hh],
                        preferred_element_type=F32)
            p_ref[slot, hh] = jnp.exp2(s).astype(BF16)

    def diagonal_probabilities(slot):
        top = pl.multiple_of(i * tk, tk)
        bottom = pl.multiple_of(i * tk + half, half)
        causal = (lax.broadcasted_iota(jnp.int32, (half, tq), 0)
                  <= lax.broadcasted_iota(jnp.int32, (half, tq), 1))
        for hh in range(2):
            q = qT_ref[0, hh]
            s = jnp.dot(k_ref[0, hh, pl.ds(top, half), :], q, preferred_element_type=F32)
            p_ref[slot, hh, 0:half, :] = jnp.exp2(jnp.where(causal, s, MASK_VALUE)).astype(BF16)
            s = jnp.dot(k_ref[0, hh, pl.ds(bottom, half), :], q[:, half:],
                        preferred_element_type=F32)
            p_ref[slot, hh, half:, half:] = jnp.exp2(
                jnp.where(causal[:, :half], s, MASK_VALUE)).astype(BF16)
            p_ref[slot, hh, half:, 0:half] = jnp.zeros((half, half), BF16)

    def accumulate(kb, slot):
        for hh in range(2):
            acc_ref[hh] += jnp.dot(vT_ref[0, hh, kb], p_ref[slot, hh],
                                   preferred_element_type=F32)

    for hh in range(2):
        acc_ref[hh] = jnp.zeros((SLAB, tq), F32)
    diagonal_probabilities(0)
    odd = i % 2

    @pl.when(odd == 1)
    def _():
        accumulate(i, 0)
        probabilities(0, 0)

    def body(t, pending):
        kb = odd + 2 * t
        accumulate(pending, 0)
        probabilities(kb, 1)
        probabilities(kb + 1, 0)
        accumulate(kb, 1)
        return kb + 1

    accumulate(lax.fori_loop(0, i // 2, body, jnp.where(odd == 1, 0, i)), 0)

    acc0 = acc_ref[0]
    acc1 = acc_ref[1]
    out0 = (acc0 * (1.0 / acc0[HEAD_DIM:HEAD_DIM + 1, :])).T
    out1 = (acc1 * (1.0 / acc1[0:1, :])).T
    lane = lax.broadcasted_iota(jnp.int32, (tq, SLAB), 1)
    o_ref[0] = jnp.where(lane < HEAD_DIM, out0, out1).astype(BF16)


def _attn_call(qT, k, vT, batch, seq, preshifted):
    nkb = seq // KV_TILE
    pairs = N_HEADS // 2
    if preshifted:
        body = _attn_preshifted_kernel
        scratch = [pltpu.VMEM((2, SLAB, Q_TILE), F32), pltpu.VMEM((2, 2, KV_TILE, Q_TILE), BF16)]
    else:
        body = _attn_kernel
        scratch = [pltpu.VMEM((2, SLAB, Q_TILE), F32), pltpu.VMEM((2, 8, Q_TILE), F32)]
    return pl.pallas_call(
        body,
        grid=(batch, pairs, seq // Q_TILE),
        in_specs=[
            pl.BlockSpec((1, 2, SLAB, Q_TILE), lambda b, j, i: (b, j, 0, i)),
            pl.BlockSpec((1, 2, seq, SLAB), lambda b, j, i: (b, j, 0, 0)),
            pl.BlockSpec((1, 2, nkb, SLAB, KV_TILE), lambda b, j, i: (b, j, 0, 0, 0)),
        ],
        out_specs=pl.BlockSpec((1, Q_TILE, SLAB), lambda b, j, i: (b, i, j)),
        out_shape=jax.ShapeDtypeStruct((batch, seq, N_HEADS * HEAD_DIM), BF16),
        scratch_shapes=scratch,
        compiler_params=_cparams(3),
        name="fox_attention_preshifted" if preshifted else "fox_attention",
    )(qT, k, vT)


def kernel(x, c, ada_w, ada_b, norm_mix_g, norm_mlp_g, sc_w_in, sc_conv, sc_w_out,
           kv_norm_g, w_kv, k_norm_g, w_f, b_f, q_w, q_norm_g, o_w, mlp_w1, mlp_w2):
    batch, seq, d = x.shape
    depth = ada_w.shape[0]
    n_conv = sc_w_in.shape[0]
    assert d == N_HEADS * HEAD_DIM and seq % TOKEN_TILE == 0 and KV_TILE % TOKEN_TILE == 0 and seq % KV_TILE == 0

    mod = _ada_call(c, ada_w, ada_b)
    sel_k, sel_q = _selection_matrices()
    q_scale = LOG2E / np.sqrt(HEAD_DIM)

    w1_all, w2_all = mlp_w1.astype(BF16), mlp_w2.astype(BF16)
    w_in_all, w_out_all = sc_w_in.astype(BF16), sc_w_out.astype(BF16)
    q_wT_all, o_w_all = jnp.swapaxes(q_w, 1, 2).astype(BF16), o_w.astype(BF16)

    xs = x.reshape(batch * seq, d)
    k = vT = fT = None
    for l in range(depth):
        if l < n_conv:
            xs = _conv_call(xs, mod[l], norm_mix_g[l], w_in_all, sc_conv[l], w_out_all, l, seq)
            xs = _mlp_call(xs, mod[l], norm_mlp_g[l], w1_all, w2_all, l, seq)
        else:
            i = l - n_conv
            q_g_cols = jnp.broadcast_to((q_norm_g[i] * q_scale)[:, None], (HEAD_DIM, TOKEN_TILE))
            bound = (np.sqrt(HEAD_DIM) * BOUND_SLACK) * jnp.max(jnp.abs(q_norm_g[i])) \
                * jnp.max(jnp.abs(k_norm_g))
            preshift_ok = bound <= MAX_PRESHIFT
            shift = jnp.where(preshift_ok, bound * LOG2E, 0.0).reshape(1).astype(F32)
            qT = _q_call(shift, xs, mod[l], norm_mix_g[l], q_wT_all, i, q_g_cols, fT,
                         sel_q, batch, seq)
            attn = lax.cond(
                preshift_ok,
                functools.partial(_attn_call, batch=batch, seq=seq, preshifted=True),
                functools.partial(_attn_call, batch=batch, seq=seq, preshifted=False),
                qT, k, vT).reshape(batch * seq, d)
            xs = _mlp_call(xs, mod[l], norm_mlp_g[l], w1_all, w2_all, l, seq,
                           attn=attn, o_w=o_w_all, o_layer=i)
        if l == n_conv - 1:
            w_f3 = jnp.zeros((d, SLAB), F32).at[:, :3 * N_HEADS].set(jnp.tile(w_f, (1, 3)))
            b_f3 = jnp.zeros((1, SLAB), F32).at[0, :3 * N_HEADS].set(jnp.tile(b_f, 3))
            k, vT, fT = _kv_call(xs, kv_norm_g, w_kv[:, :d].astype(BF16),
                                 w_kv[:, d:].T.astype(BF16), w_f3.astype(BF16), b_f3,
                                 jnp.tile(k_norm_g, 2).reshape(1, SLAB), sel_k, batch, seq)
    return xs.reshape(batch, seq, d)
```

```python
import functools

import numpy as np
import jax
import jax.numpy as jnp
from jax import lax
from jax.experimental import pallas as pl
from jax.experimental.pallas import tpu as pltpu

F32 = jnp.float32
BF16 = jnp.bfloat16

N_HEADS = 16
HEAD_DIM = 64
SLAB = 2 * HEAD_DIM
V_ROWS = HEAD_DIM + 16
N_MOD = 6
CONV_W = 3
EPS = 1e-6
MASK_VALUE = -1e30

TOKEN_TILE = 512
Q_TOKEN_TILE = 1024
FF_CHUNK = 1024
CONV_ROW_GROUPS = 1
Q_TILE = 1024
KV_TILE = 1024
ADA_COLS = 1536
LOG2E = 1.4426950408889634
MAX_PRESHIFT = 40.0
BOUND_SLACK = 1.0 + 2.0 ** -6
VMEM_LIMIT_BYTES = 56 * 1024 * 1024

NT_DIMS = (((1,), (1,)), ((), ()))
TN_DIMS = (((0,), (0,)), ((), ()))


def _cparams(n_axes):
    return pltpu.CompilerParams(
        dimension_semantics=("arbitrary",) * n_axes,
        vmem_limit_bytes=VMEM_LIMIT_BYTES)


def _resident(block_shape, index_map):
    return pl.BlockSpec(block_shape, index_map, pipeline_mode=pl.Buffered(1))


def _layer_spec(w, layer=None):
    if w.ndim == 2:
        return _resident(w.shape, lambda *_: (0, 0))
    _, rows, cols = w.shape
    return _resident((None, rows, cols), lambda *_: (layer, 0, 0))


def _with_cast_jobs(body, n_in, n_out, n_jobs):
    def kernel(*refs):
        ins, rest = refs[:n_in], refs[n_in:]
        job_src, rest = rest[:n_jobs], rest[n_jobs:]
        outs, rest = rest[:n_out], rest[n_out:]
        job_dst, scratch = rest[:n_jobs], rest[n_jobs:]
        body(*ins, *outs, *scratch)
        for src, dst in zip(job_src, job_dst):
            dst[...] = src[...].astype(BF16)
    return kernel


def _cast_job_specs(jobs, n_steps):
    in_specs, out_specs, out_shapes, operands = [], [], [], []
    for stack, layer in jobs:
        _, rows, cols = stack.shape
        slab = rows // n_steps
        assert slab * n_steps == rows and slab % 16 == 0
        in_specs.append(pl.BlockSpec((None, slab, cols), lambda t, layer=layer: (layer, t, 0)))
        out_specs.append(pl.BlockSpec((slab, cols), lambda t: (t, 0)))
        out_shapes.append(jax.ShapeDtypeStruct((rows, cols), BF16))
        operands.append(stack)
    return in_specs, out_specs, out_shapes, operands


def _rms_scale(x):
    ms = jnp.mean(x * x, axis=-1, keepdims=True)
    return x * lax.rsqrt(ms + EPS)


def _norm_mod(x, g, shift, scale):
    return (_rms_scale(x) * g) * (1.0 + scale) + shift


def _split3(v):
    hi = v.astype(BF16).astype(F32)
    r = v - hi
    mid = r.astype(BF16).astype(F32)
    lo = (r - mid).astype(BF16).astype(F32)
    return hi, mid, lo


def _ada_kernel(c_ref, w_ref, b_ref, o_ref):
    c = c_ref[...]
    ca = c * jax.nn.sigmoid(c)
    o_ref[0] = jnp.dot(ca.astype(BF16), w_ref[0].astype(BF16),
                       preferred_element_type=F32) + b_ref[0]


def _ada_call(c, ada_w, ada_b):
    depth, d, nm = ada_w.shape
    b = c.shape[0]
    rows = 8
    c_pad = jnp.zeros((rows, d), F32).at[:b].set(c)
    out = pl.pallas_call(
        _ada_kernel,
        grid=(depth, nm // ADA_COLS),
        in_specs=[
            pl.BlockSpec((rows, d), lambda l, n: (0, 0)),
            pl.BlockSpec((1, d, ADA_COLS), lambda l, n: (l, 0, n)),
            pl.BlockSpec((1, 1, ADA_COLS), lambda l, n: (l, 0, n)),
        ],
        out_specs=pl.BlockSpec((1, rows, ADA_COLS), lambda l, n: (l, 0, n)),
        out_shape=jax.ShapeDtypeStruct((depth, rows, nm), F32),
        compiler_params=_cparams(2),
        name="ada_mod",
    )(c_pad, ada_w, ada_b.reshape(depth, 1, nm))
    return out[:, :b, :].reshape(depth, b, N_MOD, 1, d)


def _conv_kernel(tiles_per_seq, x_ref, mod_ref, g_ref, win_ref, cw_ref, wout_ref,
                 o_ref, zbuf):
    tm, d = x_ref.shape
    t = pl.program_id(0)

    @pl.when(t % tiles_per_seq == 0)
    def _():
        zbuf[0:8, :] = jnp.zeros((8, d), F32)

    rg = tm // CONV_ROW_GROUPS
    for r in range(CONV_ROW_GROUPS):
        rows = slice(r * rg, (r + 1) * rg)
        x = x_ref[rows, :]
        h = _norm_mod(x, g_ref[...], mod_ref[0], mod_ref[1]).astype(BF16)
        u = jnp.dot(h, win_ref[...], preferred_element_type=F32)
        z = u[:, d:2 * d] * u[:, 2 * d:]
        zbuf[8 + r * rg:8 + (r + 1) * rg, :] = z
        z1 = zbuf[7 + r * rg:7 + (r + 1) * rg, :]
        z2 = zbuf[6 + r * rg:6 + (r + 1) * rg, :]
        zc = cw_ref[0] * z2 + cw_ref[1] * z1 + cw_ref[2] * z
        gated = (u[:, :d] * zc).astype(BF16)
        y = jnp.dot(gated, wout_ref[...], preferred_element_type=F32)
        o_ref[rows, :] = x + mod_ref[2] * y
    zbuf[0:8, :] = zbuf[tm:tm + 8, :]


def _conv_call(x, mod_l, g, w_in, conv_w, w_out, seq, cast_jobs=()):
    n, d = x.shape
    tm = TOKEN_TILE
    tps = seq // tm
    steps = n // tm
    job_in, job_out, job_shapes, job_args = _cast_job_specs(cast_jobs, steps)
    in_specs = [
        pl.BlockSpec((tm, d), lambda t: (t, 0)),
        pl.BlockSpec((None, N_MOD, 1, d), lambda t: (t // tps, 0, 0, 0)),
        _resident((1, d), lambda t: (0, 0)),
        _layer_spec(w_in),
        _resident((CONV_W, 1, d), lambda t: (0, 0, 0)),
        _layer_spec(w_out),
    ]
    outs = pl.pallas_call(
        _with_cast_jobs(functools.partial(_conv_kernel, tps), len(in_specs), 1, len(cast_jobs)),
        grid=(steps,),
        in_specs=in_specs + job_in,
        out_specs=[pl.BlockSpec((tm, d), lambda t: (t, 0))] + job_out,
        out_shape=[jax.ShapeDtypeStruct((n, d), F32)] + job_shapes,
        scratch_shapes=[pltpu.VMEM((tm + 8, d), F32)],
        compiler_params=_cparams(1),
        name="conv_mixer",
    )(x, mod_l, g.reshape(1, d), w_in, conv_w.reshape(CONV_W, 1, d), w_out, *job_args)
    return outs[0], list(outs[1:])


def _mlp_residual(x1, mod_ref, g, w1_ref, w2_ref):
    d, ff = w1_ref.shape
    h = _norm_mod(x1, g, mod_ref[3], mod_ref[4]).astype(BF16)
    acc = jnp.zeros(x1.shape, F32)
    for c in range(ff // FF_CHUNK):
        cols = slice(c * FF_CHUNK, (c + 1) * FF_CHUNK)
        a = jnp.maximum(jnp.dot(h, w1_ref[:, cols], preferred_element_type=F32), 0.0)
        acc = acc + jnp.dot((a * a).astype(BF16), w2_ref[cols, :],
                            preferred_element_type=F32)
    return x1 + mod_ref[5] * acc


def _mlp_kernel(x_ref, mod_ref, g_ref, w1_ref, w2_ref, o_ref):
    o_ref[...] = _mlp_residual(x_ref[...], mod_ref, g_ref[...], w1_ref, w2_ref)


def _omlp_kernel(x_ref, aT_ref, mod_ref, g_ref, ow_ref, w1_ref, w2_ref, o_ref):
    y = lax.dot_general(aT_ref[0], ow_ref[...], TN_DIMS, preferred_element_type=F32)
    x1 = x_ref[...] + mod_ref[2] * y
    o_ref[...] = _mlp_residual(x1, mod_ref, g_ref[...], w1_ref, w2_ref)


def _mlp_call(x, mod_l, g, w1, w2, seq, attn=None, o_w=None, o_layer=None, cast_jobs=()):
    n, d = x.shape
    tm = TOKEN_TILE
    tps = seq // tm
    steps = n // tm
    job_in, job_out, job_shapes, job_args = _cast_job_specs(cast_jobs, steps)
    tile = pl.BlockSpec((tm, d), lambda t: (t, 0))
    mod_spec = pl.BlockSpec((None, N_MOD, 1, d), lambda t: (t // tps, 0, 0, 0))
    g_spec = _resident((1, d), lambda t: (0, 0))
    w_specs = [_layer_spec(w1), _layer_spec(w2)]
    if attn is None:
        body, name = _mlp_kernel, "mlp"
        in_specs = [tile, mod_spec, g_spec] + w_specs
        args = (x, mod_l, g.reshape(1, d), w1, w2)
    else:
        body, name = _omlp_kernel, "oproj_mlp"
        attn_spec = pl.BlockSpec((1, d, tm), lambda t: (t // tps, 0, t % tps))
        in_specs = [tile, attn_spec, mod_spec, g_spec, _layer_spec(o_w, o_layer)] + w_specs
        args = (x, attn, mod_l, g.reshape(1, d), o_w, w1, w2)
    outs = pl.pallas_call(
        _with_cast_jobs(body, len(in_specs), 1, len(cast_jobs)),
        grid=(steps,),
        in_specs=in_specs + job_in,
        out_specs=[tile] + job_out,
        out_shape=[jax.ShapeDtypeStruct((n, d), F32)] + job_shapes,
        compiler_params=_cparams(1),
        name=name,
    )(*args, *job_args)
    return outs[0], list(outs[1:])


def _extra_base(head):
    return HEAD_DIM if head % 2 == 0 else 0


ONE_LANE = SLAB - 1


EXTRA_ROWS = 16


def _selection_matrices():
    sel_k = np.zeros((SLAB, N_HEADS // 2 * SLAB), np.float32)
    sel_q = np.zeros((N_HEADS * EXTRA_ROWS, SLAB), np.float32)
    for h in range(N_HEADS):
        base = (h // 2) * SLAB + _extra_base(h)
        for piece in range(3):
            sel_q[h * EXTRA_ROWS + piece, piece * N_HEADS + h] = 1.0
            sel_q[h * EXTRA_ROWS + 3 + piece, ONE_LANE] = 1.0
            sel_k[ONE_LANE, base + piece] = 1.0
            sel_k[piece * N_HEADS + h, base + 3 + piece] = -1.0
    return jnp.asarray(sel_k, BF16), jnp.asarray(sel_q, BF16)


def _pieces(f, index):
    hi, mid, lo = _split3(f)
    one = jnp.where(index == ONE_LANE, 1.0, 0.0)
    return jnp.where(index < N_HEADS, hi,
                     jnp.where(index < 2 * N_HEADS, mid,
                               jnp.where(index < 3 * N_HEADS, lo, one))).astype(BF16)


def _kv_kernel(tiles_per_seq, x_ref, g_ref, wk_ref, wvT_ref, wf_ref, bf_ref, kg_ref,
               selk_ref, tri_ref, k_ref, vT_ref, fT_ref, carry):
    tm, d = x_ref.shape
    t = pl.program_id(0)

    @pl.when(t % tiles_per_seq == 0)
    def _():
        carry[...] = jnp.zeros(carry.shape, F32)

    h = (_rms_scale(x_ref[...]) * g_ref[...]).astype(BF16)

    xf = jnp.dot(h, wf_ref[...], preferred_element_type=F32) + bf_ref[...]
    log_f = jnp.minimum(xf, 0.0) - jnp.log1p(jnp.exp(-jnp.abs(xf)))
    tri = tri_ref[...]
    f = carry[0:1, :]
    for piece in _split3(log_f):
        f = f + jnp.dot(tri, piece.astype(BF16), preferred_element_type=F32)
    carry[...] = jnp.broadcast_to(f[tm - 1:tm, :], carry.shape)
    f2 = f * LOG2E
    fT_ref[0] = f2.T

    lane = lax.broadcasted_iota(jnp.int32, (tm, SLAB), 1)
    extras = jnp.dot(_pieces(f2, lane), selk_ref[...], preferred_element_type=F32)

    k = jnp.dot(h, wk_ref[...], preferred_element_type=F32)
    low = lane < HEAD_DIM
    for j in range(N_HEADS // 2):
        ks = k[:, j * SLAB:(j + 1) * SLAB]
        k2 = ks * ks
        ms_lo = jnp.sum(jnp.where(low, k2, 0.0), axis=-1, keepdims=True) * (1.0 / HEAD_DIM)
        ms_hi = jnp.sum(jnp.where(low, 0.0, k2), axis=-1, keepdims=True) * (1.0 / HEAD_DIM)
        kn = ks * jnp.where(low, lax.rsqrt(ms_lo + EPS), lax.rsqrt(ms_hi + EPS)) * kg_ref[...]
        ex = extras[:, j * SLAB:(j + 1) * SLAB]
        k_ref[0, 2 * j] = jnp.where(low, kn, ex).astype(BF16)
        k_ref[0, 2 * j + 1] = jnp.where(low, ex, kn).astype(BF16)

    vT = lax.dot_general(wvT_ref[...], h, NT_DIMS, preferred_element_type=F32)
    row = lax.broadcasted_iota(jnp.int32, (V_ROWS - HEAD_DIM, tm), 0)
    ones_blk = jnp.where(row == 0, 1.0, 0.0).astype(BF16)
    for hd in range(N_HEADS):
        vT_ref[0, hd, 0, 0:HEAD_DIM, :] = vT[hd * HEAD_DIM:(hd + 1) * HEAD_DIM, :].astype(BF16)
        vT_ref[0, hd, 0, HEAD_DIM:V_ROWS, :] = ones_blk


def _kv_call(x, g, w_k, w_vT, w_f3, b_f3, k_g2, sel_k, batch, seq):
    n, d = x.shape
    tm = TOKEN_TILE
    tps = seq // tm
    tpk = KV_TILE // tm
    tri = jnp.asarray(np.tril(np.ones((tm, tm), np.float32)), BF16)
    return pl.pallas_call(
        functools.partial(_kv_kernel, tps),
        grid=(n // tm,),
        in_specs=[
            pl.BlockSpec((tm, d), lambda t: (t, 0)),
            _resident((1, d), lambda t: (0, 0)),
            _resident((d, d), lambda t: (0, 0)),
            _resident((d, d), lambda t: (0, 0)),
            _resident((d, SLAB), lambda t: (0, 0)),
            _resident((1, SLAB), lambda t: (0, 0)),
            _resident((1, SLAB), lambda t: (0, 0)),
            _resident((SLAB, N_HEADS // 2 * SLAB), lambda t: (0, 0)),
            _resident((tm, tm), lambda t: (0, 0)),
        ],
        out_specs=[
            pl.BlockSpec((1, N_HEADS, tm, SLAB), lambda t: (t // tps, 0, t % tps, 0)),
            pl.BlockSpec((1, N_HEADS, 1, V_ROWS, tm),
                         lambda t: (t // tps, 0, (t % tps) // tpk, 0, t % tpk)),
            pl.BlockSpec((1, SLAB, tm), lambda t: (t // tps, 0, t % tps)),
        ],
        out_shape=[
            jax.ShapeDtypeStruct((batch, N_HEADS, seq, SLAB), BF16),
            jax.ShapeDtypeStruct((batch, N_HEADS, seq // KV_TILE, V_ROWS, KV_TILE), BF16),
            jax.ShapeDtypeStruct((batch, SLAB, seq), F32),
        ],
        scratch_shapes=[pltpu.VMEM((8, SLAB), F32)],
        compiler_params=_cparams(1),
        name="shared_kv",
    )(x, g.reshape(1, d), w_k, w_vT, w_f3, b_f3, k_g2, sel_k, tri)


def _q_kernel(shift_ref, x_ref, mod_ref, g_ref, qwT_ref, qg_ref, fT_ref, selq_ref, o_ref):
    tm, d = x_ref.shape
    h = _norm_mod(x_ref[...], g_ref[...], mod_ref[0], mod_ref[1]).astype(BF16)
    qT = lax.dot_general(qwT_ref[...], h, NT_DIMS, preferred_element_type=F32)
    row = lax.broadcasted_iota(jnp.int32, (SLAB, tm), 0)
    extras = jnp.dot(selq_ref[...], _pieces(fT_ref[0] - shift_ref[0], row),
                     preferred_element_type=F32)
    unused = jnp.zeros((HEAD_DIM - EXTRA_ROWS, tm), BF16)
    for hd in range(N_HEADS):
        qh = qT[hd * HEAD_DIM:(hd + 1) * HEAD_DIM, :]
        ms = jnp.mean(qh * qh, axis=0, keepdims=True)
        qn = qh * lax.rsqrt(ms + EPS) * qg_ref[...]
        data = 0 if hd % 2 == 0 else HEAD_DIM
        other = HEAD_DIM - data
        o_ref[0, hd, data:data + HEAD_DIM, :] = qn.astype(BF16)
        o_ref[0, hd, other:other + EXTRA_ROWS, :] = (
            extras[hd * EXTRA_ROWS:(hd + 1) * EXTRA_ROWS, :].astype(BF16))
        o_ref[0, hd, other + EXTRA_ROWS:other + HEAD_DIM, :] = unused


def _q_call(shift, x, mod_l, g, q_wT, layer, q_g_cols, fT, sel_q, batch, seq):
    n, d = x.shape
    tm = Q_TOKEN_TILE
    tps = seq // tm
    return pl.pallas_call(
        _q_kernel,
        grid=(n // tm,),
        in_specs=[
            pl.BlockSpec(memory_space=pltpu.SMEM),
            pl.BlockSpec((tm, d), lambda t: (t, 0)),
            pl.BlockSpec((None, N_MOD, 1, d), lambda t: (t // tps, 0, 0, 0)),
            _resident((1, d), lambda t: (0, 0)),
            _layer_spec(q_wT, layer),
            _resident((HEAD_DIM, tm), lambda t: (0, 0)),
            pl.BlockSpec((1, SLAB, tm), lambda t: (t // tps, 0, t % tps)),
            _resident((N_HEADS * EXTRA_ROWS, SLAB), lambda t: (0, 0)),
        ],
        out_specs=pl.BlockSpec((1, N_HEADS, SLAB, tm), lambda t: (t // tps, 0, 0, t % tps)),
        out_shape=jax.ShapeDtypeStruct((batch, N_HEADS, SLAB, seq), BF16),
        compiler_params=_cparams(1),
        name="fox_query",
    )(shift, x, mod_l, g.reshape(1, d), q_wT, q_g_cols, fT, sel_q)


def _store_normalized(acc_ref, o_ref):
    for hh in range(2):
        inv = 1.0 / acc_ref[hh, HEAD_DIM:HEAD_DIM + 1, :]
        o_ref[0, hh * HEAD_DIM:(hh + 1) * HEAD_DIM, :] = (
            acc_ref[hh, 0:HEAD_DIM, :] * inv).astype(BF16)


def _attn_kernel(qT_ref, k_ref, vT_ref, o_ref, acc_ref, m_ref):
    tq = qT_ref.shape[-1]
    tk = vT_ref.shape[-1]
    i = pl.program_id(2)
    n_kv = (i * tq + tq + tk - 1) // tk

    for hh in range(2):
        acc_ref[hh] = jnp.zeros((V_ROWS, tq), F32)
        m_ref[hh] = jnp.full((8, tq), MASK_VALUE, F32)

    def step(kb, masked):
        start = pl.multiple_of(kb * tk, tk)
        for hh in range(2):
            s = jnp.dot(k_ref[0, hh, pl.ds(start, tk), :], qT_ref[0, hh],
                        preferred_element_type=F32)
            if masked:
                key_pos = kb * tk + lax.broadcasted_iota(jnp.int32, (tk, tq), 0)
                q_pos = i * tq + lax.broadcasted_iota(jnp.int32, (tk, tq), 1)
                s = jnp.where(key_pos <= q_pos, s, MASK_VALUE)
            m_old = m_ref[hh, 0:1, :]
            m_new = jnp.maximum(m_old, jnp.max(s, axis=0, keepdims=True))
            alpha = jnp.exp2(m_old - m_new)
            p = jnp.exp2(s - m_new).astype(BF16)
            pv = jnp.dot(vT_ref[0, hh, kb], p, preferred_element_type=F32)
            acc_ref[hh] = alpha * acc_ref[hh] + pv
            m_ref[hh] = jnp.broadcast_to(m_new, (8, tq))

    def body(kb, carry):
        step(kb, False)
        return carry

    lax.fori_loop(0, n_kv - 1, body, 0)
    step(n_kv - 1, True)

    _store_normalized(acc_ref, o_ref)


def _attn_preshifted_kernel(qT_ref, k_ref, vT_ref, o_ref, acc_ref, p_ref):
    tq = qT_ref.shape[-1]
    tk = vT_ref.shape[-1]
    assert tq == tk
    half = tk // 2
    i = pl.program_id(2)

    def probabilities(kb, slot):
        start = pl.multiple_of(kb * tk, tk)
        for hh in range(2):
            s = jnp.dot(k_ref[0, hh, pl.ds(start, tk), :], qT_ref[0, hh],
                        preferred_element_type=F32)
            p_ref[slot, hh] = jnp.exp2(s).astype(BF16)

    def diagonal_probabilities(slot):
        top = pl.multiple_of(i * tk, tk)
        bottom = pl.multiple_of(i * tk + half, half)
        causal = (lax.broadcasted_iota(jnp.int32, (half, tq), 0)
                  <= lax.broadcasted_iota(jnp.int32, (half, tq), 1))
        for hh in range(2):
            q = qT_ref[0, hh]
            s = jnp.dot(k_ref[0, hh, pl.ds(top, half), :], q, preferred_element_type=F32)
            p_ref[slot, hh, 0:half, :] = jnp.exp2(jnp.where(causal, s, MASK_VALUE)).astype(BF16)
            s = jnp.dot(k_ref[0, hh, pl.ds(bottom, half), :], q[:, half:],
                        preferred_element_type=F32)
            p_ref[slot, hh, half:, half:] = jnp.exp2(
                jnp.where(causal[:, :half], s, MASK_VALUE)).astype(BF16)
            p_ref[slot, hh, half:, 0:half] = jnp.zeros((half, half), BF16)

    def accumulate(kb, slot):
        for hh in range(2):
            acc_ref[hh] += jnp.dot(vT_ref[0, hh, kb], p_ref[slot, hh],
                                   preferred_element_type=F32)

    for hh in range(2):
        acc_ref[hh] = jnp.zeros((V_ROWS, tq), F32)
    diagonal_probabilities(0)
    odd = i % 2

    @pl.when(odd == 1)
    def _():
        accumulate(i, 0)
        probabilities(0, 0)

    def body(t, pending):
        kb = odd + 2 * t
        accumulate(pending, 0)
        probabilities(kb, 1)
        probabilities(kb + 1, 0)
        accumulate(kb, 1)
        return kb + 1

    accumulate(lax.fori_loop(0, i // 2, body, jnp.where(odd == 1, 0, i)), 0)

    _store_normalized(acc_ref, o_ref)


def _attn_call(qT, k, vT, batch, seq, preshifted):
    nkb = seq // KV_TILE
    pairs = N_HEADS // 2
    if preshifted:
        body = _attn_preshifted_kernel
        scratch = [pltpu.VMEM((2, V_ROWS, Q_TILE), F32), pltpu.VMEM((2, 2, KV_TILE, Q_TILE), BF16)]
    else:
        body = _attn_kernel
        scratch = [pltpu.VMEM((2, V_ROWS, Q_TILE), F32), pltpu.VMEM((2, 8, Q_TILE), F32)]
    return pl.pallas_call(
        body,
        grid=(batch, pairs, seq // Q_TILE),
        in_specs=[
            pl.BlockSpec((1, 2, SLAB, Q_TILE), lambda b, j, i: (b, j, 0, i)),
            pl.BlockSpec((1, 2, seq, SLAB), lambda b, j, i: (b, j, 0, 0)),
            pl.BlockSpec((1, 2, nkb, V_ROWS, KV_TILE), lambda b, j, i: (b, j, 0, 0, 0)),
        ],
        out_specs=pl.BlockSpec((1, SLAB, Q_TILE), lambda b, j, i: (b, j, i)),
        out_shape=jax.ShapeDtypeStruct((batch, N_HEADS * HEAD_DIM, seq), BF16),
        scratch_shapes=scratch,
        compiler_params=_cparams(3),
        name="fox_attention_preshifted" if preshifted else "fox_attention",
    )(qT, k, vT)


def kernel(x, c, ada_w, ada_b, norm_mix_g, norm_mlp_g, sc_w_in, sc_conv, sc_w_out,
           kv_norm_g, w_kv, k_norm_g, w_f, b_f, q_w, q_norm_g, o_w, mlp_w1, mlp_w2):
    batch, seq, d = x.shape
    depth = ada_w.shape[0]
    n_conv = sc_w_in.shape[0]
    assert d == N_HEADS * HEAD_DIM and seq % KV_TILE == 0 and seq % Q_TOKEN_TILE == 0
    assert KV_TILE % TOKEN_TILE == 0 and KV_TILE == Q_TILE

    mod = _ada_call(c, ada_w, ada_b)
    sel_k, sel_q = _selection_matrices()
    q_scale = LOG2E / np.sqrt(HEAD_DIM)

    q_wT_all, o_w_all = jnp.swapaxes(q_w, 1, 2).astype(BF16), o_w.astype(BF16)
    conv_w_bf16 = {0: [sc_w_in[0].astype(BF16), sc_w_out[0].astype(BF16)]} if n_conv else {}
    mlp_w_bf16 = {} if n_conv else {0: [mlp_w1[0].astype(BF16), mlp_w2[0].astype(BF16)]}

    def jobs_for_next(l):
        if l + 1 >= depth:
            return []
        return [(sc_w_in, l + 1), (sc_w_out, l + 1)] if l + 1 < n_conv \
            else [(mlp_w1, l + 1), (mlp_w2, l + 1)]

    def keep_casts(l, casts):
        if casts:
            (conv_w_bf16 if l + 1 < n_conv else mlp_w_bf16)[l + 1] = casts

    xs = x.reshape(batch * seq, d)
    k = vT = fT = None
    for l in range(depth):
        if l < n_conv:
            xs, mlp_w_bf16[l] = _conv_call(xs, mod[l], norm_mix_g[l], conv_w_bf16[l][0], sc_conv[l],
                                           conv_w_bf16[l][1], seq,
                                           cast_jobs=[(mlp_w1, l), (mlp_w2, l)])
            xs, casts = _mlp_call(xs, mod[l], norm_mlp_g[l], *mlp_w_bf16[l], seq,
                                  cast_jobs=jobs_for_next(l))
            keep_casts(l, casts)
        else:
            i = l - n_conv
            q_g_cols = jnp.broadcast_to((q_norm_g[i] * q_scale)[:, None], (HEAD_DIM, Q_TOKEN_TILE))
            bound = (np.sqrt(HEAD_DIM) * BOUND_SLACK) * jnp.max(jnp.abs(q_norm_g[i])) \
                * jnp.max(jnp.abs(k_norm_g))
            preshift_ok = bound <= MAX_PRESHIFT
            shift = jnp.where(preshift_ok, bound * LOG2E, 0.0).reshape(1).astype(F32)
            qT = _q_call(shift, xs, mod[l], norm_mix_g[l], q_wT_all, i, q_g_cols, fT,
                         sel_q, batch, seq)
            attn = lax.cond(
                preshift_ok,
                functools.partial(_attn_call, batch=batch, seq=seq, preshifted=True),
                functools.partial(_attn_call, batch=batch, seq=seq, preshifted=False),
                qT, k, vT)
            xs, casts = _mlp_call(xs, mod[l], norm_mlp_g[l], *mlp_w_bf16[l], seq,
                                  attn=attn, o_w=o_w_all, o_layer=i, cast_jobs=jobs_for_next(l))
            keep_casts(l, casts)
        if l == n_conv - 1:
            w_f3 = jnp.zeros((d, SLAB), F32).at[:, :3 * N_HEADS].set(jnp.tile(w_f, (1, 3)))
            b_f3 = jnp.zeros((1, SLAB), F32).at[0, :3 * N_HEADS].set(jnp.tile(b_f, 3))
            k, vT, fT = _kv_call(xs, kv_norm_g, w_kv[:, :d].astype(BF16),
                                 w_kv[:, d:].T.astype(BF16), w_f3.astype(BF16), b_f3,
                                 jnp.tile(k_norm_g, 2).reshape(1, SLAB), sel_k, batch, seq)
    return xs.reshape(batch, seq, d)
```

```python
import functools

import numpy as np
import jax
import jax.numpy as jnp
from jax import lax
from jax.experimental import pallas as pl
from jax.experimental.pallas import tpu as pltpu

F32 = jnp.float32
BF16 = jnp.bfloat16

N_HEADS = 16
HEAD_DIM = 64
SLAB = 2 * HEAD_DIM
V_ROWS = 2 * HEAD_DIM
N_MOD = 6
CONV_W = 3
EPS = 1e-6
MASK_VALUE = -1e30

TOKEN_TILE = 512
Q_TOKEN_TILE = 1024
FF_CHUNK = 1024
CONV_ROW_GROUPS = 1
Q_TILE = 1024
KV_TILE = 1024
ADA_COLS = 1536
LOG2E = 1.4426950408889634
MAX_PRESHIFT = 40.0
BOUND_SLACK = 1.0 + 2.0 ** -6
VMEM_LIMIT_BYTES = 56 * 1024 * 1024

NT_DIMS = (((1,), (1,)), ((), ()))
TN_DIMS = (((0,), (0,)), ((), ()))


def _cparams(n_axes):
    return pltpu.CompilerParams(
        dimension_semantics=("arbitrary",) * n_axes,
        vmem_limit_bytes=VMEM_LIMIT_BYTES)


def _resident(block_shape, index_map):
    return pl.BlockSpec(block_shape, index_map, pipeline_mode=pl.Buffered(1))


def _layer_spec(w, layer=None):
    if w.ndim == 2:
        return _resident(w.shape, lambda *_: (0, 0))
    _, rows, cols = w.shape
    return _resident((None, rows, cols), lambda *_: (layer, 0, 0))


def _with_cast_jobs(body, n_in, n_out, n_jobs):
    def kernel(*refs):
        ins, rest = refs[:n_in], refs[n_in:]
        job_src, rest = rest[:n_jobs], rest[n_jobs:]
        outs, rest = rest[:n_out], rest[n_out:]
        job_dst, scratch = rest[:n_jobs], rest[n_jobs:]
        body(*ins, *outs, *scratch)
        for src, dst in zip(job_src, job_dst):
            dst[...] = src[...].astype(BF16)
    return kernel


def _cast_job_specs(jobs, n_steps):
    in_specs, out_specs, out_shapes, operands = [], [], [], []
    for stack, layer in jobs:
        _, rows, cols = stack.shape
        slab = rows // n_steps
        assert slab * n_steps == rows and slab % 16 == 0
        in_specs.append(pl.BlockSpec((None, slab, cols), lambda t, layer=layer: (layer, t, 0)))
        out_specs.append(pl.BlockSpec((slab, cols), lambda t: (t, 0)))
        out_shapes.append(jax.ShapeDtypeStruct((rows, cols), BF16))
        operands.append(stack)
    return in_specs, out_specs, out_shapes, operands


def _rms_scale(x):
    ms = jnp.mean(x * x, axis=-1, keepdims=True)
    return x * lax.rsqrt(ms + EPS)


def _norm_mod(x, g, shift, scale):
    return (_rms_scale(x) * g) * (1.0 + scale) + shift


def _split3(v):
    hi = v.astype(BF16).astype(F32)
    r = v - hi
    mid = r.astype(BF16).astype(F32)
    lo = (r - mid).astype(BF16).astype(F32)
    return hi, mid, lo


def _ada_kernel(c_ref, w_ref, b_ref, o_ref):
    c = c_ref[...]
    ca = c * jax.nn.sigmoid(c)
    o_ref[0] = jnp.dot(ca.astype(BF16), w_ref[0].astype(BF16),
                       preferred_element_type=F32) + b_ref[0]


def _ada_call(c, ada_w, ada_b):
    depth, d, nm = ada_w.shape
    b = c.shape[0]
    rows = 8
    c_pad = jnp.zeros((rows, d), F32).at[:b].set(c)
    out = pl.pallas_call(
        _ada_kernel,
        grid=(depth, nm // ADA_COLS),
        in_specs=[
            pl.BlockSpec((rows, d), lambda l, n: (0, 0)),
            pl.BlockSpec((1, d, ADA_COLS), lambda l, n: (l, 0, n)),
            pl.BlockSpec((1, 1, ADA_COLS), lambda l, n: (l, 0, n)),
        ],
        out_specs=pl.BlockSpec((1, rows, ADA_COLS), lambda l, n: (l, 0, n)),
        out_shape=jax.ShapeDtypeStruct((depth, rows, nm), F32),
        compiler_params=_cparams(2),
        name="ada_mod",
    )(c_pad, ada_w, ada_b.reshape(depth, 1, nm))
    return out[:, :b, :].reshape(depth, b, N_MOD, 1, d)


def _conv_kernel(tiles_per_seq, x_ref, mod_ref, g_ref, win_ref, cw_ref, wout_ref,
                 o_ref, zbuf):
    tm, d = x_ref.shape
    t = pl.program_id(0)

    @pl.when(t % tiles_per_seq == 0)
    def _():
        zbuf[0:8, :] = jnp.zeros((8, d), F32)

    rg = tm // CONV_ROW_GROUPS
    for r in range(CONV_ROW_GROUPS):
        rows = slice(r * rg, (r + 1) * rg)
        x = x_ref[rows, :]
        h = _norm_mod(x, g_ref[...], mod_ref[0], mod_ref[1]).astype(BF16)
        u = jnp.dot(h, win_ref[...], preferred_element_type=F32)
        z = u[:, d:2 * d] * u[:, 2 * d:]
        zbuf[8 + r * rg:8 + (r + 1) * rg, :] = z
        z1 = zbuf[7 + r * rg:7 + (r + 1) * rg, :]
        z2 = zbuf[6 + r * rg:6 + (r + 1) * rg, :]
        zc = cw_ref[0] * z2 + cw_ref[1] * z1 + cw_ref[2] * z
        gated = (u[:, :d] * zc).astype(BF16)
        y = jnp.dot(gated, wout_ref[...], preferred_element_type=F32)
        o_ref[rows, :] = x + mod_ref[2] * y
    zbuf[0:8, :] = zbuf[tm:tm + 8, :]


def _conv_call(x, mod_l, g, w_in, conv_w, w_out, seq, cast_jobs=()):
    n, d = x.shape
    tm = TOKEN_TILE
    tps = seq // tm
    steps = n // tm
    job_in, job_out, job_shapes, job_args = _cast_job_specs(cast_jobs, steps)
    in_specs = [
        pl.BlockSpec((tm, d), lambda t: (t, 0)),
        pl.BlockSpec((None, N_MOD, 1, d), lambda t: (t // tps, 0, 0, 0)),
        _resident((1, d), lambda t: (0, 0)),
        _layer_spec(w_in),
        _resident((CONV_W, 1, d), lambda t: (0, 0, 0)),
        _layer_spec(w_out),
    ]
    outs = pl.pallas_call(
        _with_cast_jobs(functools.partial(_conv_kernel, tps), len(in_specs), 1, len(cast_jobs)),
        grid=(steps,),
        in_specs=in_specs + job_in,
        out_specs=[pl.BlockSpec((tm, d), lambda t: (t, 0))] + job_out,
        out_shape=[jax.ShapeDtypeStruct((n, d), F32)] + job_shapes,
        scratch_shapes=[pltpu.VMEM((tm + 8, d), F32)],
        compiler_params=_cparams(1),
        name="conv_mixer",
    )(x, mod_l, g.reshape(1, d), w_in, conv_w.reshape(CONV_W, 1, d), w_out, *job_args)
    return outs[0], list(outs[1:])


def _mlp_residual(x1, mod_ref, g, w1_ref, w2_ref):
    d, ff = w1_ref.shape
    h = _norm_mod(x1, g, mod_ref[3], mod_ref[4]).astype(BF16)
    acc = jnp.zeros(x1.shape, F32)
    for c in range(ff // FF_CHUNK):
        cols = slice(c * FF_CHUNK, (c + 1) * FF_CHUNK)
        a = jnp.maximum(jnp.dot(h, w1_ref[:, cols], preferred_element_type=F32), 0.0)
        acc = acc + jnp.dot((a * a).astype(BF16), w2_ref[cols, :],
                            preferred_element_type=F32)
    return x1 + mod_ref[5] * acc


def _mlp_kernel(x_ref, mod_ref, g_ref, w1_ref, w2_ref, o_ref):
    o_ref[...] = _mlp_residual(x_ref[...], mod_ref, g_ref[...], w1_ref, w2_ref)


def _omlp_kernel(x_ref, aT_ref, mod_ref, g_ref, ow_ref, w1_ref, w2_ref, o_ref):
    y = lax.dot_general(aT_ref[0], ow_ref[...], TN_DIMS, preferred_element_type=F32)
    x1 = x_ref[...] + mod_ref[2] * y
    o_ref[...] = _mlp_residual(x1, mod_ref, g_ref[...], w1_ref, w2_ref)


def _mlp_call(x, mod_l, g, w1, w2, seq, attn=None, o_w=None, o_layer=None, cast_jobs=()):
    n, d = x.shape
    tm = TOKEN_TILE
    tps = seq // tm
    steps = n // tm
    job_in, job_out, job_shapes, job_args = _cast_job_specs(cast_jobs, steps)
    tile = pl.BlockSpec((tm, d), lambda t: (t, 0))
    mod_spec = pl.BlockSpec((None, N_MOD, 1, d), lambda t: (t // tps, 0, 0, 0))
    g_spec = _resident((1, d), lambda t: (0, 0))
    w_specs = [_layer_spec(w1), _layer_spec(w2)]
    if attn is None:
        body, name = _mlp_kernel, "mlp"
        in_specs = [tile, mod_spec, g_spec] + w_specs
        args = (x, mod_l, g.reshape(1, d), w1, w2)
    else:
        body, name = _omlp_kernel, "oproj_mlp"
        attn_spec = pl.BlockSpec((1, d, tm), lambda t: (t // tps, 0, t % tps))
        in_specs = [tile, attn_spec, mod_spec, g_spec, _layer_spec(o_w, o_layer)] + w_specs
        args = (x, attn, mod_l, g.reshape(1, d), o_w, w1, w2)
    outs = pl.pallas_call(
        _with_cast_jobs(body, len(in_specs), 1, len(cast_jobs)),
        grid=(steps,),
        in_specs=in_specs + job_in,
        out_specs=[tile] + job_out,
        out_shape=[jax.ShapeDtypeStruct((n, d), F32)] + job_shapes,
        compiler_params=_cparams(1),
        name=name,
    )(*args, *job_args)
    return outs[0], list(outs[1:])


def _extra_base(head):
    return HEAD_DIM if head % 2 == 0 else 0


ONE_LANE = SLAB - 1


EXTRA_ROWS = 16


def _selection_matrices():
    sel_k = np.zeros((SLAB, N_HEADS // 2 * SLAB), np.float32)
    sel_q = np.zeros((N_HEADS * EXTRA_ROWS, SLAB), np.float32)
    for h in range(N_HEADS):
        base = (h // 2) * SLAB + _extra_base(h)
        for piece in range(3):
            sel_q[h * EXTRA_ROWS + piece, piece * N_HEADS + h] = 1.0
            sel_q[h * EXTRA_ROWS + 3 + piece, ONE_LANE] = 1.0
            sel_k[ONE_LANE, base + piece] = 1.0
            sel_k[piece * N_HEADS + h, base + 3 + piece] = -1.0
    return jnp.asarray(sel_k, BF16), jnp.asarray(sel_q, BF16)


def _pieces(f, index):
    hi, mid, lo = _split3(f)
    one = jnp.where(index == ONE_LANE, 1.0, 0.0)
    return jnp.where(index < N_HEADS, hi,
                     jnp.where(index < 2 * N_HEADS, mid,
                               jnp.where(index < 3 * N_HEADS, lo, one))).astype(BF16)


def _kv_kernel(tiles_per_seq, x_ref, g_ref, wk_ref, wvT_ref, wf_ref, bf_ref, kg_ref,
               selk_ref, tri_ref, k_ref, vT_ref, fT_ref, carry):
    tm, d = x_ref.shape
    t = pl.program_id(0)

    @pl.when(t % tiles_per_seq == 0)
    def _():
        carry[...] = jnp.zeros(carry.shape, F32)

    h = (_rms_scale(x_ref[...]) * g_ref[...]).astype(BF16)

    xf = jnp.dot(h, wf_ref[...], preferred_element_type=F32) + bf_ref[...]
    log_f = jnp.minimum(xf, 0.0) - jnp.log1p(jnp.exp(-jnp.abs(xf)))
    tri = tri_ref[...]
    f = carry[0:1, :]
    for piece in _split3(log_f):
        f = f + jnp.dot(tri, piece.astype(BF16), preferred_element_type=F32)
    carry[...] = jnp.broadcast_to(f[tm - 1:tm, :], carry.shape)
    f2 = f * LOG2E
    fT_ref[0] = f2.T

    lane = lax.broadcasted_iota(jnp.int32, (tm, SLAB), 1)
    extras = jnp.dot(_pieces(f2, lane), selk_ref[...], preferred_element_type=F32)

    k = jnp.dot(h, wk_ref[...], preferred_element_type=F32)
    low = lane < HEAD_DIM
    for j in range(N_HEADS // 2):
        ks = k[:, j * SLAB:(j + 1) * SLAB]
        k2 = ks * ks
        ms_lo = jnp.sum(jnp.where(low, k2, 0.0), axis=-1, keepdims=True) * (1.0 / HEAD_DIM)
        ms_hi = jnp.sum(jnp.where(low, 0.0, k2), axis=-1, keepdims=True) * (1.0 / HEAD_DIM)
        kn = ks * jnp.where(low, lax.rsqrt(ms_lo + EPS), lax.rsqrt(ms_hi + EPS)) * kg_ref[...]
        ex = extras[:, j * SLAB:(j + 1) * SLAB]
        k_ref[0, 2 * j] = jnp.where(low, kn, ex).astype(BF16)
        k_ref[0, 2 * j + 1] = jnp.where(low, ex, kn).astype(BF16)

    vT = lax.dot_general(wvT_ref[...], h, NT_DIMS, preferred_element_type=F32)
    row = lax.broadcasted_iota(jnp.int32, (V_ROWS - HEAD_DIM, tm), 0)
    ones_blk = jnp.where(row == 0, 1.0, 0.0).astype(BF16)
    for hd in range(N_HEADS):
        vT_ref[0, hd, 0, 0:HEAD_DIM, :] = vT[hd * HEAD_DIM:(hd + 1) * HEAD_DIM, :].astype(BF16)
        vT_ref[0, hd, 0, HEAD_DIM:V_ROWS, :] = ones_blk


def _kv_call(x, g, w_k, w_vT, w_f3, b_f3, k_g2, sel_k, batch, seq):
    n, d = x.shape
    tm = TOKEN_TILE
    tps = seq // tm
    tpk = KV_TILE // tm
    tri = jnp.asarray(np.tril(np.ones((tm, tm), np.float32)), BF16)
    return pl.pallas_call(
        functools.partial(_kv_kernel, tps),
        grid=(n // tm,),
        in_specs=[
            pl.BlockSpec((tm, d), lambda t: (t, 0)),
            _resident((1, d), lambda t: (0, 0)),
            _resident((d, d), lambda t: (0, 0)),
            _resident((d, d), lambda t: (0, 0)),
            _resident((d, SLAB), lambda t: (0, 0)),
            _resident((1, SLAB), lambda t: (0, 0)),
            _resident((1, SLAB), lambda t: (0, 0)),
            _resident((SLAB, N_HEADS // 2 * SLAB), lambda t: (0, 0)),
            _resident((tm, tm), lambda t: (0, 0)),
        ],
        out_specs=[
            pl.BlockSpec((1, N_HEADS, tm, SLAB), lambda t: (t // tps, 0, t % tps, 0)),
            pl.BlockSpec((1, N_HEADS, 1, V_ROWS, tm),
                         lambda t: (t // tps, 0, (t % tps) // tpk, 0, t % tpk)),
            pl.BlockSpec((1, SLAB, tm), lambda t: (t // tps, 0, t % tps)),
        ],
        out_shape=[
            jax.ShapeDtypeStruct((batch, N_HEADS, seq, SLAB), BF16),
            jax.ShapeDtypeStruct((batch, N_HEADS, seq // KV_TILE, V_ROWS, KV_TILE), BF16),
            jax.ShapeDtypeStruct((batch, SLAB, seq), F32),
        ],
        scratch_shapes=[pltpu.VMEM((8, SLAB), F32)],
        compiler_params=_cparams(1),
        name="shared_kv",
    )(x, g.reshape(1, d), w_k, w_vT, w_f3, b_f3, k_g2, sel_k, tri)


def _q_kernel(shift_ref, x_ref, mod_ref, g_ref, qwT_ref, qg_ref, fT_ref, selq_ref, o_ref):
    tm, d = x_ref.shape
    h = _norm_mod(x_ref[...], g_ref[...], mod_ref[0], mod_ref[1]).astype(BF16)
    qT = lax.dot_general(qwT_ref[...], h, NT_DIMS, preferred_element_type=F32)
    row = lax.broadcasted_iota(jnp.int32, (SLAB, tm), 0)
    extras = jnp.dot(selq_ref[...], _pieces(fT_ref[0] - shift_ref[0], row),
                     preferred_element_type=F32)
    unused = jnp.zeros((HEAD_DIM - EXTRA_ROWS, tm), BF16)
    for hd in range(N_HEADS):
        qh = qT[hd * HEAD_DIM:(hd + 1) * HEAD_DIM, :]
        ms = jnp.mean(qh * qh, axis=0, keepdims=True)
        qn = qh * lax.rsqrt(ms + EPS) * qg_ref[...]
        data = 0 if hd % 2 == 0 else HEAD_DIM
        other = HEAD_DIM - data
        o_ref[0, hd, data:data + HEAD_DIM, :] = qn.astype(BF16)
        o_ref[0, hd, other:other + EXTRA_ROWS, :] = (
            extras[hd * EXTRA_ROWS:(hd + 1) * EXTRA_ROWS, :].astype(BF16))
        o_ref[0, hd, other + EXTRA_ROWS:other + HEAD_DIM, :] = unused


def _q_call(shift, x, mod_l, g, q_wT, layer, q_g_cols, fT, sel_q, batch, seq):
    n, d = x.shape
    tm = Q_TOKEN_TILE
    tps = seq // tm
    return pl.pallas_call(
        _q_kernel,
        grid=(n // tm,),
        in_specs=[
            pl.BlockSpec(memory_space=pltpu.SMEM),
            pl.BlockSpec((tm, d), lambda t: (t, 0)),
            pl.BlockSpec((None, N_MOD, 1, d), lambda t: (t // tps, 0, 0, 0)),
            _resident((1, d), lambda t: (0, 0)),
            _layer_spec(q_wT, layer),
            _resident((HEAD_DIM, tm), lambda t: (0, 0)),
            pl.BlockSpec((1, SLAB, tm), lambda t: (t // tps, 0, t % tps)),
            _resident((N_HEADS * EXTRA_ROWS, SLAB), lambda t: (0, 0)),
        ],
        out_specs=pl.BlockSpec((1, N_HEADS, SLAB, tm), lambda t: (t // tps, 0, 0, t % tps)),
        out_shape=jax.ShapeDtypeStruct((batch, N_HEADS, SLAB, seq), BF16),
        compiler_params=_cparams(1),
        name="fox_query",
    )(shift, x, mod_l, g.reshape(1, d), q_wT, q_g_cols, fT, sel_q)


def _store_normalized(acc_ref, o_ref):
    for hh in range(2):
        inv = 1.0 / acc_ref[hh, HEAD_DIM:HEAD_DIM + 1, :]
        o_ref[0, hh * HEAD_DIM:(hh + 1) * HEAD_DIM, :] = (
            acc_ref[hh, 0:HEAD_DIM, :] * inv).astype(BF16)


def _attn_kernel(qT_ref, k_ref, vT_ref, o_ref, acc_ref, m_ref):
    tq = qT_ref.shape[-1]
    tk = vT_ref.shape[-1]
    i = pl.program_id(2)
    n_kv = (i * tq + tq + tk - 1) // tk

    for hh in range(2):
        acc_ref[hh] = jnp.zeros((V_ROWS, tq), F32)
        m_ref[hh] = jnp.full((8, tq), MASK_VALUE, F32)

    def step(kb, masked):
        start = pl.multiple_of(kb * tk, tk)
        for hh in range(2):
            s = jnp.dot(k_ref[0, hh, pl.ds(start, tk), :], qT_ref[0, hh],
                        preferred_element_type=F32)
            if masked:
                key_pos = kb * tk + lax.broadcasted_iota(jnp.int32, (tk, tq), 0)
                q_pos = i * tq + lax.broadcasted_iota(jnp.int32, (tk, tq), 1)
                s = jnp.where(key_pos <= q_pos, s, MASK_VALUE)
            m_old = m_ref[hh, 0:1, :]
            m_new = jnp.maximum(m_old, jnp.max(s, axis=0, keepdims=True))
            alpha = jnp.exp2(m_old - m_new)
            p = jnp.exp2(s - m_new).astype(BF16)
            pv = jnp.dot(vT_ref[0, hh, kb], p, preferred_element_type=F32)
            acc_ref[hh] = alpha * acc_ref[hh] + pv
            m_ref[hh] = jnp.broadcast_to(m_new, (8, tq))

    def body(kb, carry):
        step(kb, False)
        return carry

    lax.fori_loop(0, n_kv - 1, body, 0)
    step(n_kv - 1, True)

    _store_normalized(acc_ref, o_ref)


def _attn_preshifted_kernel(qT_ref, k_ref, vT_ref, o_ref, acc_ref, p_ref):
    tq = qT_ref.shape[-1]
    tk = vT_ref.shape[-1]
    assert tq == tk
    half = tk // 2
    i = pl.program_id(2)

    def probabilities(kb, slot):
        start = pl.multiple_of(kb * tk, tk)
        for hh in range(2):
            s = jnp.dot(k_ref[0, hh, pl.ds(start, tk), :], qT_ref[0, hh],
                        preferred_element_type=F32)
            p_ref[slot, hh] = jnp.exp2(s).astype(BF16)

    def diagonal_probabilities(slot):
        top = pl.multiple_of(i * tk, tk)
        bottom = pl.multiple_of(i * tk + half, half)
        causal = (lax.broadcasted_iota(jnp.int32, (half, tq), 0)
                  <= lax.broadcasted_iota(jnp.int32, (half, tq), 1))
        for hh in range(2):
            q = qT_ref[0, hh]
            s = jnp.dot(k_ref[0, hh, pl.ds(top, half), :], q, preferred_element_type=F32)
            p_ref[slot, hh, 0:half, :] = jnp.exp2(jnp.where(causal, s, MASK_VALUE)).astype(BF16)
            s = jnp.dot(k_ref[0, hh, pl.ds(bottom, half), :], q[:, half:],
                        preferred_element_type=F32)
            p_ref[slot, hh, half:, half:] = jnp.exp2(
                jnp.where(causal[:, :half], s, MASK_VALUE)).astype(BF16)
            p_ref[slot, hh, half:, 0:half] = jnp.zeros((half, half), BF16)

    def accumulate(kb, slot):
        for hh in range(2):
            acc_ref[hh] += jnp.dot(vT_ref[0, hh, kb], p_ref[slot, hh],
                                   preferred_element_type=F32)

    for hh in range(2):
        acc_ref[hh] = jnp.zeros((V_ROWS, tq), F32)
    diagonal_probabilities(0)
    odd = i % 2

    @pl.when(odd == 1)
    def _():
        accumulate(i, 0)
        probabilities(0, 0)

    def body(t, pending):
        kb = odd + 2 * t
        accumulate(pending, 0)
        probabilities(kb, 1)
        probabilities(kb + 1, 0)
        accumulate(kb, 1)
        return kb + 1

    accumulate(lax.fori_loop(0, i // 2, body, jnp.where(odd == 1, 0, i)), 0)

    _store_normalized(acc_ref, o_ref)


def _attn_call(qT, k, vT, batch, seq, preshifted):
    nkb = seq // KV_TILE
    pairs = N_HEADS // 2
    if preshifted:
        body = _attn_preshifted_kernel
        scratch = [pltpu.VMEM((2, V_ROWS, Q_TILE), F32), pltpu.VMEM((2, 2, KV_TILE, Q_TILE), BF16)]
    else:
        body = _attn_kernel
        scratch = [pltpu.VMEM((2, V_ROWS, Q_TILE), F32), pltpu.VMEM((2, 8, Q_TILE), F32)]
    return pl.pallas_call(
        body,
        grid=(batch, pairs, seq // Q_TILE),
        in_specs=[
            pl.BlockSpec((1, 2, SLAB, Q_TILE), lambda b, j, i: (b, j, 0, i)),
            pl.BlockSpec((1, 2, seq, SLAB), lambda b, j, i: (b, j, 0, 0)),
            pl.BlockSpec((1, 2, nkb, V_ROWS, KV_TILE), lambda b, j, i: (b, j, 0, 0, 0)),
        ],
        out_specs=pl.BlockSpec((1, SLAB, Q_TILE), lambda b, j, i: (b, j, i)),
        out_shape=jax.ShapeDtypeStruct((batch, N_HEADS * HEAD_DIM, seq), BF16),
        scratch_shapes=scratch,
        compiler_params=_cparams(3),
        name="fox_attention_preshifted" if preshifted else "fox_attention",
    )(qT, k, vT)


def kernel(x, c, ada_w, ada_b, norm_mix_g, norm_mlp_g, sc_w_in, sc_conv, sc_w_out,
           kv_norm_g, w_kv, k_norm_g, w_f, b_f, q_w, q_norm_g, o_w, mlp_w1, mlp_w2):
    batch, seq, d = x.shape
    depth = ada_w.shape[0]
    n_conv = sc_w_in.shape[0]
    assert d == N_HEADS * HEAD_DIM and seq % KV_TILE == 0 and seq % Q_TOKEN_TILE == 0
    assert KV_TILE % TOKEN_TILE == 0 and KV_TILE == Q_TILE

    mod = _ada_call(c, ada_w, ada_b)
    sel_k, sel_q = _selection_matrices()
    q_scale = LOG2E / np.sqrt(HEAD_DIM)

    q_wT_all, o_w_all = jnp.swapaxes(q_w, 1, 2).astype(BF16), o_w.astype(BF16)
    conv_w_bf16 = {0: [sc_w_in[0].astype(BF16), sc_w_out[0].astype(BF16)]} if n_conv else {}
    mlp_w_bf16 = {} if n_conv else {0: [mlp_w1[0].astype(BF16), mlp_w2[0].astype(BF16)]}

    def jobs_for_next(l):
        if l + 1 >= depth:
            return []
        return [(sc_w_in, l + 1), (sc_w_out, l + 1)] if l + 1 < n_conv \
            else [(mlp_w1, l + 1), (mlp_w2, l + 1)]

    def keep_casts(l, casts):
        if casts:
            (conv_w_bf16 if l + 1 < n_conv else mlp_w_bf16)[l + 1] = casts

    xs = x.reshape(batch * seq, d)
    k = vT = fT = None
    for l in range(depth):
        if l < n_conv:
            xs, mlp_w_bf16[l] = _conv_call(xs, mod[l], norm_mix_g[l], conv_w_bf16[l][0], sc_conv[l],
                                           conv_w_bf16[l][1], seq,
                                           cast_jobs=[(mlp_w1, l), (mlp_w2, l)])
            xs, casts = _mlp_call(xs, mod[l], norm_mlp_g[l], *mlp_w_bf16[l], seq,
                                  cast_jobs=jobs_for_next(l))
            keep_casts(l, casts)
        else:
            i = l - n_conv
            q_g_cols = jnp.broadcast_to((q_norm_g[i] * q_scale)[:, None], (HEAD_DIM, Q_TOKEN_TILE))
            bound = (np.sqrt(HEAD_DIM) * BOUND_SLACK) * jnp.max(jnp.abs(q_norm_g[i])) \
                * jnp.max(jnp.abs(k_norm_g))
            preshift_ok = bound <= MAX_PRESHIFT
            shift = jnp.where(preshift_ok, bound * LOG2E, 0.0).reshape(1).astype(F32)
            qT = _q_call(shift, xs, mod[l], norm_mix_g[l], q_wT_all, i, q_g_cols, fT,
                         sel_q, batch, seq)
            attn = lax.cond(
                preshift_ok,
                functools.partial(_attn_call, batch=batch, seq=seq, preshifted=True),
                functools.partial(_attn_call, batch=batch, seq=seq, preshifted=False),
                qT, k, vT)
            xs, casts = _mlp_call(xs, mod[l], norm_mlp_g[l], *mlp_w_bf16[l], seq,
                                  attn=attn, o_w=o_w_all, o_layer=i, cast_jobs=jobs_for_next(l))
            keep_casts(l, casts)
        if l == n_conv - 1:
            w_f3 = jnp.zeros((d, SLAB), F32).at[:, :3 * N_HEADS].set(jnp.tile(w_f, (1, 3)))
            b_f3 = jnp.zeros((1, SLAB), F32).at[0, :3 * N_HEADS].set(jnp.tile(b_f, 3))
            k, vT, fT = _kv_call(xs, kv_norm_g, w_kv[:, :d].astype(BF16),
                                 w_kv[:, d:].T.astype(BF16), w_f3.astype(BF16), b_f3,
                                 jnp.tile(k_norm_g, 2).reshape(1, SLAB), sel_k, batch, seq)
    return xs.reshape(batch, seq, d)
```

```python
import functools

import numpy as np
import jax
import jax.numpy as jnp
from jax import lax
from jax.experimental import pallas as pl
from jax.experimental.pallas import tpu as pltpu

F32 = jnp.float32
BF16 = jnp.bfloat16

SUBLANES = 8
N_HEADS = 16
HEAD_DIM = 64
SLAB = 2 * HEAD_DIM
V_ROWS = 2 * HEAD_DIM
N_MOD = 6
CONV_W = 3
EPS = 1e-6
MASK_VALUE = -1e30

TOKEN_TILE = 512
Q_TOKEN_TILE = 1024
FF_CHUNK = 1024
Q_TILE = 1024
KV_TILE = 1024
DIAG_BAND = 256
ADA_COLS = 1536
LOG2E = 1.4426950408889634
MAX_PRESHIFT = 40.0
BOUND_SLACK = 1.0 + 2.0 ** -6
VMEM_LIMIT_BYTES = 56 * 1024 * 1024

NT_DIMS = (((1,), (1,)), ((), ()))
TN_DIMS = (((0,), (0,)), ((), ()))


def _cparams(n_axes):
    return pltpu.CompilerParams(
        dimension_semantics=("arbitrary",) * n_axes,
        vmem_limit_bytes=VMEM_LIMIT_BYTES)


def _resident(block_shape, index_map):
    return pl.BlockSpec(block_shape, index_map, pipeline_mode=pl.Buffered(1))


def _layer_spec(w, layer=None):
    if w.ndim == 2:
        return _resident(w.shape, lambda *_: (0, 0))
    _, rows, cols = w.shape
    return _resident((None, rows, cols), lambda *_: (layer, 0, 0))


def _with_cast_jobs(body, n_in, n_out, n_jobs):
    def kernel(*refs):
        ins, rest = refs[:n_in], refs[n_in:]
        job_src, rest = rest[:n_jobs], rest[n_jobs:]
        outs, rest = rest[:n_out], rest[n_out:]
        job_dst, scratch = rest[:n_jobs], rest[n_jobs:]
        body(*ins, *outs, *scratch)
        for src, dst in zip(job_src, job_dst):
            dst[...] = src[...].astype(BF16)
    return kernel


def _cast_job_specs(jobs, n_steps):
    in_specs, out_specs, out_shapes, operands = [], [], [], []
    for stack, layer in jobs:
        _, rows, cols = stack.shape
        slab = rows // n_steps
        assert slab * n_steps == rows and slab % 16 == 0
        in_specs.append(pl.BlockSpec((None, slab, cols), lambda t, layer=layer: (layer, t, 0)))
        out_specs.append(pl.BlockSpec((slab, cols), lambda t: (t, 0)))
        out_shapes.append(jax.ShapeDtypeStruct((rows, cols), BF16))
        operands.append(stack)
    return in_specs, out_specs, out_shapes, operands


def _rms_scale(x):
    ms = jnp.mean(x * x, axis=-1, keepdims=True)
    return x * lax.rsqrt(ms + EPS)


def _norm_mod(x, g, shift, scale):
    return (_rms_scale(x) * g) * (1.0 + scale) + shift


def _split3(v):
    hi = v.astype(BF16).astype(F32)
    r = v - hi
    mid = r.astype(BF16).astype(F32)
    lo = (r - mid).astype(BF16).astype(F32)
    return hi, mid, lo


def _ada_kernel(c_ref, w_ref, b_ref, o_ref):
    c = c_ref[...]
    ca = c * jax.nn.sigmoid(c)
    o_ref[0] = jnp.dot(ca.astype(BF16), w_ref[0].astype(BF16),
                       preferred_element_type=F32) + b_ref[0]


def _ada_call(c, ada_w, ada_b):
    depth, d, nm = ada_w.shape
    b = c.shape[0]
    rows = SUBLANES
    c_pad = jnp.zeros((rows, d), F32).at[:b].set(c)
    out = pl.pallas_call(
        _ada_kernel,
        grid=(depth, nm // ADA_COLS),
        in_specs=[
            pl.BlockSpec((rows, d), lambda l, n: (0, 0)),
            pl.BlockSpec((1, d, ADA_COLS), lambda l, n: (l, 0, n)),
            pl.BlockSpec((1, 1, ADA_COLS), lambda l, n: (l, 0, n)),
        ],
        out_specs=pl.BlockSpec((1, rows, ADA_COLS), lambda l, n: (l, 0, n)),
        out_shape=jax.ShapeDtypeStruct((depth, rows, nm), F32),
        compiler_params=_cparams(2),
        name="ada_mod",
    )(c_pad, ada_w, ada_b.reshape(depth, 1, nm))
    return out[:, :b, :].reshape(depth, b, N_MOD, 1, d)


def _conv_kernel(tiles_per_seq, x_ref, mod_ref, g_ref, win_ref, cw_ref, wout_ref,
                 o_ref, zbuf):
    tm, d = x_ref.shape
    t = pl.program_id(0)

    @pl.when(t % tiles_per_seq == 0)
    def _():
        zbuf[0:SUBLANES, :] = jnp.zeros((SUBLANES, d), F32)

    x = x_ref[...]
    h = _norm_mod(x, g_ref[...], mod_ref[0], mod_ref[1]).astype(BF16)
    u = jnp.dot(h, win_ref[...], preferred_element_type=F32)
    z = u[:, d:2 * d] * u[:, 2 * d:]
    zbuf[SUBLANES:SUBLANES + tm, :] = z
    z1 = zbuf[SUBLANES - 1:SUBLANES - 1 + tm, :]
    z2 = zbuf[SUBLANES - 2:SUBLANES - 2 + tm, :]
    zc = cw_ref[0] * z2 + cw_ref[1] * z1 + cw_ref[2] * z
    zbuf[0:SUBLANES, :] = z[tm - SUBLANES:, :]
    gated = (u[:, :d] * zc).astype(BF16)
    y = jnp.dot(gated, wout_ref[...], preferred_element_type=F32)
    o_ref[...] = x + mod_ref[2] * y


def _conv_call(x, mod_l, g, w_in, conv_w, w_out, seq, cast_jobs=()):
    n, d = x.shape
    tm = TOKEN_TILE
    tps = seq // tm
    steps = n // tm
    job_in, job_out, job_shapes, job_args = _cast_job_specs(cast_jobs, steps)
    in_specs = [
        pl.BlockSpec((tm, d), lambda t: (t, 0)),
        pl.BlockSpec((None, N_MOD, 1, d), lambda t: (t // tps, 0, 0, 0)),
        _resident((1, d), lambda t: (0, 0)),
        _layer_spec(w_in),
        _resident((CONV_W, 1, d), lambda t: (0, 0, 0)),
        _layer_spec(w_out),
    ]
    outs = pl.pallas_call(
        _with_cast_jobs(functools.partial(_conv_kernel, tps), len(in_specs), 1, len(cast_jobs)),
        grid=(steps,),
        in_specs=in_specs + job_in,
        out_specs=[pl.BlockSpec((tm, d), lambda t: (t, 0))] + job_out,
        out_shape=[jax.ShapeDtypeStruct((n, d), F32)] + job_shapes,
        scratch_shapes=[pltpu.VMEM((tm + SUBLANES, d), F32)],
        compiler_params=_cparams(1),
        name="conv_mixer",
    )(x, mod_l, g.reshape(1, d), w_in, conv_w.reshape(CONV_W, 1, d), w_out, *job_args)
    return outs[0], list(outs[1:])


def _mlp_residual(x1, mod_ref, g, w1_ref, w2_ref):
    d, ff = w1_ref.shape
    h = _norm_mod(x1, g, mod_ref[3], mod_ref[4]).astype(BF16)
    n_chunks = ff // FF_CHUNK

    def up(c):
        return jnp.dot(h, w1_ref[:, c * FF_CHUNK:(c + 1) * FF_CHUNK], preferred_element_type=F32)

    acc = jnp.zeros(x1.shape, F32)
    pre = up(0)
    for c in range(n_chunks):
        nxt = up(c + 1) if c + 1 < n_chunks else None
        a = jnp.maximum(pre, 0.0)
        acc = acc + jnp.dot((a * a).astype(BF16), w2_ref[c * FF_CHUNK:(c + 1) * FF_CHUNK, :],
                            preferred_element_type=F32)
        pre = nxt
    return x1 + mod_ref[5] * acc


def _mlp_kernel(x_ref, mod_ref, g_ref, w1_ref, w2_ref, o_ref):
    o_ref[...] = _mlp_residual(x_ref[...], mod_ref, g_ref[...], w1_ref, w2_ref)


def _omlp_kernel(x_ref, aT_ref, mod_ref, g_ref, ow_ref, w1_ref, w2_ref, o_ref):
    y = lax.dot_general(aT_ref[0], ow_ref[...], TN_DIMS, preferred_element_type=F32)
    x1 = x_ref[...] + mod_ref[2] * y
    o_ref[...] = _mlp_residual(x1, mod_ref, g_ref[...], w1_ref, w2_ref)


def _mlp_call(x, mod_l, g, w1, w2, seq, attn=None, o_w=None, o_layer=None, cast_jobs=()):
    n, d = x.shape
    tm = TOKEN_TILE
    tps = seq // tm
    steps = n // tm
    job_in, job_out, job_shapes, job_args = _cast_job_specs(cast_jobs, steps)
    tile = pl.BlockSpec((tm, d), lambda t: (t, 0))
    mod_spec = pl.BlockSpec((None, N_MOD, 1, d), lambda t: (t // tps, 0, 0, 0))
    g_spec = _resident((1, d), lambda t: (0, 0))
    w_specs = [_layer_spec(w1), _layer_spec(w2)]
    if attn is None:
        body, name = _mlp_kernel, "mlp"
        in_specs = [tile, mod_spec, g_spec] + w_specs
        args = (x, mod_l, g.reshape(1, d), w1, w2)
    else:
        body, name = _omlp_kernel, "oproj_mlp"
        attn_spec = pl.BlockSpec((1, d, tm), lambda t: (t // tps, 0, t % tps))
        in_specs = [tile, attn_spec, mod_spec, g_spec, _layer_spec(o_w, o_layer)] + w_specs
        args = (x, attn, mod_l, g.reshape(1, d), o_w, w1, w2)
    outs = pl.pallas_call(
        _with_cast_jobs(body, len(in_specs), 1, len(cast_jobs)),
        grid=(steps,),
        in_specs=in_specs + job_in,
        out_specs=[tile] + job_out,
        out_shape=[jax.ShapeDtypeStruct((n, d), F32)] + job_shapes,
        compiler_params=_cparams(1),
        name=name,
    )(*args, *job_args)
    return outs[0], list(outs[1:])


def _extra_base(head):
    return HEAD_DIM if head % 2 == 0 else 0


ONE_LANE = SLAB - 1


EXTRA_ROWS = 16


def _selection_matrices():
    sel_k = np.zeros((SLAB, N_HEADS // 2 * SLAB), np.float32)
    sel_q = np.zeros((N_HEADS * EXTRA_ROWS, SLAB), np.float32)
    for h in range(N_HEADS):
        base = (h // 2) * SLAB + _extra_base(h)
        for piece in range(3):
            sel_q[h * EXTRA_ROWS + piece, piece * N_HEADS + h] = 1.0
            sel_q[h * EXTRA_ROWS + 3 + piece, ONE_LANE] = 1.0
            sel_k[ONE_LANE, base + piece] = 1.0
            sel_k[piece * N_HEADS + h, base + 3 + piece] = -1.0
    return jnp.asarray(sel_k, BF16), jnp.asarray(sel_q, BF16)


def _pieces(f, index):
    hi, mid, lo = _split3(f)
    one = jnp.where(index == ONE_LANE, 1.0, 0.0)
    return jnp.where(index < N_HEADS, hi,
                     jnp.where(index < 2 * N_HEADS, mid,
                               jnp.where(index < 3 * N_HEADS, lo, one))).astype(BF16)


def _kv_kernel(tiles_per_seq, x_ref, g_ref, wk_ref, wvT_ref, wf_ref, bf_ref, kg_ref,
               selk_ref, tri_ref, k_ref, vT_ref, fT_ref, carry):
    tm, d = x_ref.shape
    t = pl.program_id(0)

    @pl.when(t % tiles_per_seq == 0)
    def _():
        carry[...] = jnp.zeros(carry.shape, F32)

    h = (_rms_scale(x_ref[...]) * g_ref[...]).astype(BF16)

    xf = jnp.dot(h, wf_ref[...], preferred_element_type=F32) + bf_ref[...]
    k = jnp.dot(h, wk_ref[...], preferred_element_type=F32)
    log_f = jnp.minimum(xf, 0.0) - jnp.log1p(jnp.exp(-jnp.abs(xf)))
    pieces = jnp.concatenate([p.astype(BF16) for p in _split3(log_f)], axis=-1)
    sums = jnp.dot(tri_ref[...], pieces, preferred_element_type=F32)
    vT = lax.dot_general(wvT_ref[...], h, NT_DIMS, preferred_element_type=F32)
    f = carry[0:1, :] + sums[:, 0:SLAB] + sums[:, SLAB:2 * SLAB] + sums[:, 2 * SLAB:]
    carry[...] = jnp.broadcast_to(f[tm - 1:tm, :], carry.shape)
    f2 = f * LOG2E
    fT_ref[0] = f2.T

    lane = lax.broadcasted_iota(jnp.int32, (tm, SLAB), 1)
    extras = jnp.dot(_pieces(f2, lane), selk_ref[...], preferred_element_type=F32)

    low = lane < HEAD_DIM
    for j in range(N_HEADS // 2):
        ks = k[:, j * SLAB:(j + 1) * SLAB]
        k2 = ks * ks
        ms_lo = jnp.sum(jnp.where(low, k2, 0.0), axis=-1, keepdims=True) * (1.0 / HEAD_DIM)
        ms_hi = jnp.sum(jnp.where(low, 0.0, k2), axis=-1, keepdims=True) * (1.0 / HEAD_DIM)
        kn = ks * jnp.where(low, lax.rsqrt(ms_lo + EPS), lax.rsqrt(ms_hi + EPS)) * kg_ref[...]
        ex = extras[:, j * SLAB:(j + 1) * SLAB]
        k_ref[0, 2 * j] = jnp.where(low, kn, ex).astype(BF16)
        k_ref[0, 2 * j + 1] = jnp.where(low, ex, kn).astype(BF16)

    row = lax.broadcasted_iota(jnp.int32, (V_ROWS - HEAD_DIM, tm), 0)
    ones_blk = jnp.where(row == 0, 1.0, 0.0).astype(BF16)
    for hd in range(N_HEADS):
        vT_ref[0, hd, 0, 0:HEAD_DIM, :] = vT[hd * HEAD_DIM:(hd + 1) * HEAD_DIM, :].astype(BF16)
        vT_ref[0, hd, 0, HEAD_DIM:V_ROWS, :] = ones_blk


def _kv_call(x, g, w_k, w_vT, w_f3, b_f3, k_g2, sel_k, batch, seq):
    n, d = x.shape
    tm = TOKEN_TILE
    tps = seq // tm
    tpk = KV_TILE // tm
    tri = jnp.asarray(np.tril(np.ones((tm, tm), np.float32)), BF16)
    return pl.pallas_call(
        functools.partial(_kv_kernel, tps),
        grid=(n // tm,),
        in_specs=[
            pl.BlockSpec((tm, d), lambda t: (t, 0)),
            _resident((1, d), lambda t: (0, 0)),
            _resident((d, d), lambda t: (0, 0)),
            _resident((d, d), lambda t: (0, 0)),
            _resident((d, SLAB), lambda t: (0, 0)),
            _resident((1, SLAB), lambda t: (0, 0)),
            _resident((1, SLAB), lambda t: (0, 0)),
            _resident((SLAB, N_HEADS // 2 * SLAB), lambda t: (0, 0)),
            _resident((tm, tm), lambda t: (0, 0)),
        ],
        out_specs=[
            pl.BlockSpec((1, N_HEADS, tm, SLAB), lambda t: (t // tps, 0, t % tps, 0)),
            pl.BlockSpec((1, N_HEADS, 1, V_ROWS, tm),
                         lambda t: (t // tps, 0, (t % tps) // tpk, 0, t % tpk)),
            pl.BlockSpec((1, SLAB, tm), lambda t: (t // tps, 0, t % tps)),
        ],
        out_shape=[
            jax.ShapeDtypeStruct((batch, N_HEADS, seq, SLAB), BF16),
            jax.ShapeDtypeStruct((batch, N_HEADS, seq // KV_TILE, V_ROWS, KV_TILE), BF16),
            jax.ShapeDtypeStruct((batch, SLAB, seq), F32),
        ],
        scratch_shapes=[pltpu.VMEM((SUBLANES, SLAB), F32)],
        compiler_params=_cparams(1),
        name="shared_kv",
    )(x, g.reshape(1, d), w_k, w_vT, w_f3, b_f3, k_g2, sel_k, tri)


def _q_kernel(shift_ref, x_ref, mod_ref, g_ref, qwT_ref, qg_ref, fT_ref, selq_ref, o_ref):
    tm, d = x_ref.shape
    row = lax.broadcasted_iota(jnp.int32, (SLAB, tm), 0)
    extras = jnp.dot(selq_ref[...], _pieces(fT_ref[0] - shift_ref[0], row),
                     preferred_element_type=F32)
    h = _norm_mod(x_ref[...], g_ref[...], mod_ref[0], mod_ref[1]).astype(BF16)
    qT = lax.dot_general(qwT_ref[...], h, NT_DIMS, preferred_element_type=F32)
    unused = jnp.zeros((HEAD_DIM - EXTRA_ROWS, tm), BF16)
    for hd in range(N_HEADS):
        qh = qT[hd * HEAD_DIM:(hd + 1) * HEAD_DIM, :]
        ms = jnp.mean(qh * qh, axis=0, keepdims=True)
        qn = qh * lax.rsqrt(ms + EPS) * qg_ref[...]
        data = 0 if hd % 2 == 0 else HEAD_DIM
        other = HEAD_DIM - data
        o_ref[0, hd, data:data + HEAD_DIM, :] = qn.astype(BF16)
        o_ref[0, hd, other:other + EXTRA_ROWS, :] = (
            extras[hd * EXTRA_ROWS:(hd + 1) * EXTRA_ROWS, :].astype(BF16))
        o_ref[0, hd, other + EXTRA_ROWS:other + HEAD_DIM, :] = unused


def _q_call(shift, x, mod_l, g, q_wT, layer, q_g_cols, fT, sel_q, batch, seq):
    n, d = x.shape
    tm = Q_TOKEN_TILE
    tps = seq // tm
    return pl.pallas_call(
        _q_kernel,
        grid=(n // tm,),
        in_specs=[
            pl.BlockSpec(memory_space=pltpu.SMEM),
            pl.BlockSpec((tm, d), lambda t: (t, 0)),
            pl.BlockSpec((None, N_MOD, 1, d), lambda t: (t // tps, 0, 0, 0)),
            _resident((1, d), lambda t: (0, 0)),
            _layer_spec(q_wT, layer),
            _resident((HEAD_DIM, tm), lambda t: (0, 0)),
            pl.BlockSpec((1, SLAB, tm), lambda t: (t // tps, 0, t % tps)),
            _resident((N_HEADS * EXTRA_ROWS, SLAB), lambda t: (0, 0)),
        ],
        out_specs=pl.BlockSpec((1, N_HEADS, SLAB, tm), lambda t: (t // tps, 0, 0, t % tps)),
        out_shape=jax.ShapeDtypeStruct((batch, N_HEADS, SLAB, seq), BF16),
        compiler_params=_cparams(1),
        name="fox_query",
    )(shift, x, mod_l, g.reshape(1, d), q_wT, q_g_cols, fT, sel_q)


def _store_normalized(acc_ref, o_ref):
    for hh in range(2):
        inv = 1.0 / acc_ref[hh, HEAD_DIM:HEAD_DIM + 1, :]
        o_ref[0, hh * HEAD_DIM:(hh + 1) * HEAD_DIM, :] = (
            acc_ref[hh, 0:HEAD_DIM, :] * inv).astype(BF16)


def _attn_kernel(qT_ref, k_ref, vT_ref, o_ref, acc_ref, m_ref):
    tq = qT_ref.shape[-1]
    tk = vT_ref.shape[-1]
    i = pl.program_id(2)
    n_kv = (i * tq + tq + tk - 1) // tk

    for hh in range(2):
        acc_ref[hh] = jnp.zeros((V_ROWS, tq), F32)
        m_ref[hh] = jnp.full((SUBLANES, tq), MASK_VALUE, F32)

    def step(kb, masked):
        start = pl.multiple_of(kb * tk, tk)
        for hh in range(2):
            s = jnp.dot(k_ref[0, hh, pl.ds(start, tk), :], qT_ref[0, hh],
                        preferred_element_type=F32)
            if masked:
                key_pos = kb * tk + lax.broadcasted_iota(jnp.int32, (tk, tq), 0)
                q_pos = i * tq + lax.broadcasted_iota(jnp.int32, (tk, tq), 1)
                s = jnp.where(key_pos <= q_pos, s, MASK_VALUE)
            m_old = m_ref[hh, 0:1, :]
            m_new = jnp.maximum(m_old, jnp.max(s, axis=0, keepdims=True))
            alpha = jnp.exp2(m_old - m_new)
            p = jnp.exp2(s - m_new).astype(BF16)
            pv = jnp.dot(vT_ref[0, hh, kb], p, preferred_element_type=F32)
            acc_ref[hh] = alpha * acc_ref[hh] + pv
            m_ref[hh] = jnp.broadcast_to(m_new, (SUBLANES, tq))

    def body(kb, carry):
        step(kb, False)
        return carry

    lax.fori_loop(0, n_kv - 1, body, 0)
    step(n_kv - 1, True)

    _store_normalized(acc_ref, o_ref)


def _attn_preshifted_kernel(qT_ref, k_ref, vT_ref, o_ref, acc_ref, p_ref):
    tq = qT_ref.shape[-1]
    tk = vT_ref.shape[-1]
    assert tq == tk
    i = pl.program_id(2)

    def probabilities(kb, slot):
        start = pl.multiple_of(kb * tk, tk)
        for hh in range(2):
            s = jnp.dot(k_ref[0, hh, pl.ds(start, tk), :], qT_ref[0, hh],
                        preferred_element_type=F32)
            p_ref[slot, hh] = jnp.exp2(s).astype(BF16)

    def diagonal_probabilities(slot):
        causal = (lax.broadcasted_iota(jnp.int32, (DIAG_BAND, tq), 0)
                  <= lax.broadcasted_iota(jnp.int32, (DIAG_BAND, tq), 1))
        for hh in range(2):
            q = qT_ref[0, hh]
            for r in range(tk // DIAG_BAND):
                lo = r * DIAG_BAND
                start = pl.multiple_of(i * tk + lo, DIAG_BAND)
                s = jnp.dot(k_ref[0, hh, pl.ds(start, DIAG_BAND), :], q[:, lo:],
                            preferred_element_type=F32)
                p_ref[slot, hh, lo:lo + DIAG_BAND, lo:] = jnp.exp2(
                    jnp.where(causal[:, :tq - lo], s, MASK_VALUE)).astype(BF16)
                if lo:
                    p_ref[slot, hh, lo:lo + DIAG_BAND, 0:lo] = jnp.zeros((DIAG_BAND, lo), BF16)

    def accumulate(kb, slot):
        for hh in range(2):
            acc_ref[hh] += jnp.dot(vT_ref[0, hh, kb], p_ref[slot, hh],
                                   preferred_element_type=F32)

    for hh in range(2):
        acc_ref[hh] = jnp.zeros((V_ROWS, tq), F32)
    diagonal_probabilities(0)
    odd = i % 2

    @pl.when(odd == 1)
    def _():
        accumulate(i, 0)
        probabilities(0, 0)

    def body(t, pending):
        kb = odd + 2 * t
        accumulate(pending, 0)
        probabilities(kb, 1)
        probabilities(kb + 1, 0)
        accumulate(kb, 1)
        return kb + 1

    accumulate(lax.fori_loop(0, i // 2, body, jnp.where(odd == 1, 0, i)), 0)

    _store_normalized(acc_ref, o_ref)


def _attn_call(qT, k, vT, batch, seq, preshifted):
    nkb = seq // KV_TILE
    pairs = N_HEADS // 2
    if preshifted:
        body = _attn_preshifted_kernel
        scratch = [pltpu.VMEM((2, V_ROWS, Q_TILE), F32), pltpu.VMEM((2, 2, KV_TILE, Q_TILE), BF16)]
    else:
        body = _attn_kernel
        scratch = [pltpu.VMEM((2, V_ROWS, Q_TILE), F32), pltpu.VMEM((2, SUBLANES, Q_TILE), F32)]
    return pl.pallas_call(
        body,
        grid=(batch, pairs, seq // Q_TILE),
        in_specs=[
            pl.BlockSpec((1, 2, SLAB, Q_TILE), lambda b, j, i: (b, j, 0, i)),
            pl.BlockSpec((1, 2, seq, SLAB), lambda b, j, i: (b, j, 0, 0)),
            pl.BlockSpec((1, 2, nkb, V_ROWS, KV_TILE), lambda b, j, i: (b, j, 0, 0, 0)),
        ],
        out_specs=pl.BlockSpec((1, SLAB, Q_TILE), lambda b, j, i: (b, j, i)),
        out_shape=jax.ShapeDtypeStruct((batch, N_HEADS * HEAD_DIM, seq), BF16),
        scratch_shapes=scratch,
        compiler_params=_cparams(3),
        name="fox_attention_preshifted" if preshifted else "fox_attention",
    )(qT, k, vT)


def kernel(x, c, ada_w, ada_b, norm_mix_g, norm_mlp_g, sc_w_in, sc_conv, sc_w_out,
           kv_norm_g, w_kv, k_norm_g, w_f, b_f, q_w, q_norm_g, o_w, mlp_w1, mlp_w2):
    batch, seq, d = x.shape
    depth = ada_w.shape[0]
    n_conv = sc_w_in.shape[0]
    assert d == N_HEADS * HEAD_DIM and seq % KV_TILE == 0 and seq % Q_TOKEN_TILE == 0
    assert KV_TILE % TOKEN_TILE == 0 and KV_TILE == Q_TILE

    mod = _ada_call(c, ada_w, ada_b)
    sel_k, sel_q = _selection_matrices()
    q_scale = LOG2E / np.sqrt(HEAD_DIM)

    q_wT_all, o_w_all = jnp.swapaxes(q_w, 1, 2).astype(BF16), o_w.astype(BF16)
    conv_w_bf16 = {0: [sc_w_in[0].astype(BF16), sc_w_out[0].astype(BF16)]} if n_conv else {}
    mlp_w_bf16 = {} if n_conv else {0: [mlp_w1[0].astype(BF16), mlp_w2[0].astype(BF16)]}

    def jobs_for_next(l):
        if l + 1 >= depth:
            return []
        return [(sc_w_in, l + 1), (sc_w_out, l + 1)] if l + 1 < n_conv \
            else [(mlp_w1, l + 1), (mlp_w2, l + 1)]

    def keep_casts(l, casts):
        if casts:
            (conv_w_bf16 if l + 1 < n_conv else mlp_w_bf16)[l + 1] = casts

    xs = x.reshape(batch * seq, d)
    k = vT = fT = None
    for l in range(depth):
        if l < n_conv:
            xs, mlp_w_bf16[l] = _conv_call(xs, mod[l], norm_mix_g[l], conv_w_bf16[l][0], sc_conv[l],
                                           conv_w_bf16[l][1], seq,
                                           cast_jobs=[(mlp_w1, l), (mlp_w2, l)])
            xs, casts = _mlp_call(xs, mod[l], norm_mlp_g[l], *mlp_w_bf16[l], seq,
                                  cast_jobs=jobs_for_next(l))
            keep_casts(l, casts)
        else:
            i = l - n_conv
            q_g_cols = jnp.broadcast_to((q_norm_g[i] * q_scale)[:, None], (HEAD_DIM, Q_TOKEN_TILE))
            bound = (np.sqrt(HEAD_DIM) * BOUND_SLACK) * jnp.max(jnp.abs(q_norm_g[i])) \
                * jnp.max(jnp.abs(k_norm_g))
            preshift_ok = bound <= MAX_PRESHIFT
            shift = jnp.where(preshift_ok, bound * LOG2E, 0.0).reshape(1).astype(F32)
            qT = _q_call(shift, xs, mod[l], norm_mix_g[l], q_wT_all, i, q_g_cols, fT,
                         sel_q, batch, seq)
            attn = lax.cond(
                preshift_ok,
                functools.partial(_attn_call, batch=batch, seq=seq, preshifted=True),
                functools.partial(_attn_call, batch=batch, seq=seq, preshifted=False),
                qT, k, vT)
            xs, casts = _mlp_call(xs, mod[l], norm_mlp_g[l], *mlp_w_bf16[l], seq,
                                  attn=attn, o_w=o_w_all, o_layer=i, cast_jobs=jobs_for_next(l))
            keep_casts(l, casts)
        if l == n_conv - 1:
            w_f3 = jnp.zeros((d, SLAB), F32).at[:, :3 * N_HEADS].set(jnp.tile(w_f, (1, 3)))
            b_f3 = jnp.zeros((1, SLAB), F32).at[0, :3 * N_HEADS].set(jnp.tile(b_f, 3))
            k, vT, fT = _kv_call(xs, kv_norm_g, w_kv[:, :d].astype(BF16),
                                 w_kv[:, d:].T.astype(BF16), w_f3.astype(BF16), b_f3,
                                 jnp.tile(k_norm_g, 2).reshape(1, SLAB), sel_k, batch, seq)
    return xs.reshape(batch, seq, d)
```

```python
import functools

import numpy as np
import jax
import jax.numpy as jnp
from jax import lax
from jax.experimental import pallas as pl
from jax.experimental.pallas import tpu as pltpu

F32 = jnp.float32
BF16 = jnp.bfloat16

SUBLANES = 8
N_HEADS = 16
HEAD_DIM = 64
SLAB = 2 * HEAD_DIM
V_ROWS = 2 * HEAD_DIM
N_MOD = 6
CONV_W = 3
EPS = 1e-6
MASK_VALUE = -1e30

TOKEN_TILE = 512
Q_TOKEN_TILE = 1024
FF_CHUNK = 1024
Q_TILE = 1024
KV_TILE = 1024
DIAG_BAND = 256
ADA_COLS = 1536
LOG2E = 1.4426950408889634
MAX_PRESHIFT = 40.0
BOUND_SLACK = 1.0 + 2.0 ** -6
VMEM_LIMIT_BYTES = 56 * 1024 * 1024

NT_DIMS = (((1,), (1,)), ((), ()))
TN_DIMS = (((0,), (0,)), ((), ()))


def _cparams(n_axes):
    return pltpu.CompilerParams(
        dimension_semantics=("arbitrary",) * n_axes,
        vmem_limit_bytes=VMEM_LIMIT_BYTES)


def _resident(block_shape, index_map):
    return pl.BlockSpec(block_shape, index_map, pipeline_mode=pl.Buffered(1))


def _layer_spec(w, layer=None):
    if w.ndim == 2:
        return _resident(w.shape, lambda *_: (0, 0))
    _, rows, cols = w.shape
    return _resident((None, rows, cols), lambda *_: (layer, 0, 0))


def _with_cast_jobs(body, n_in, n_out, n_jobs):
    def kernel(*refs):
        ins, rest = refs[:n_in], refs[n_in:]
        job_src, rest = rest[:n_jobs], rest[n_jobs:]
        outs, rest = rest[:n_out], rest[n_out:]
        job_dst, scratch = rest[:n_jobs], rest[n_jobs:]
        body(*ins, *outs, *scratch)
        for src, dst in zip(job_src, job_dst):
            dst[...] = src[...].astype(BF16)
    return kernel


def _cast_job_specs(jobs, n_steps):
    in_specs, out_specs, out_shapes, operands = [], [], [], []
    for stack, layer in jobs:
        _, rows, cols = stack.shape
        slab = rows // n_steps
        assert slab * n_steps == rows and slab % 16 == 0
        in_specs.append(pl.BlockSpec((None, slab, cols), lambda t, layer=layer: (layer, t, 0)))
        out_specs.append(pl.BlockSpec((slab, cols), lambda t: (t, 0)))
        out_shapes.append(jax.ShapeDtypeStruct((rows, cols), BF16))
        operands.append(stack)
    return in_specs, out_specs, out_shapes, operands


def _rms_scale(x):
    ms = jnp.mean(x * x, axis=-1, keepdims=True)
    return x * lax.rsqrt(ms + EPS)


def _norm_mod(x, g, shift, scale):
    return (_rms_scale(x) * g) * (1.0 + scale) + shift


def _split3(v):
    hi = v.astype(BF16).astype(F32)
    r = v - hi
    mid = r.astype(BF16).astype(F32)
    lo = (r - mid).astype(BF16).astype(F32)
    return hi, mid, lo


def _ada_kernel(c_ref, w_ref, b_ref, o_ref):
    c = c_ref[...]
    ca = c * jax.nn.sigmoid(c)
    o_ref[0] = jnp.dot(ca.astype(BF16), w_ref[0].astype(BF16),
                       preferred_element_type=F32) + b_ref[0]


def _ada_call(c, ada_w, ada_b):
    depth, d, nm = ada_w.shape
    b = c.shape[0]
    rows = SUBLANES
    c_pad = jnp.zeros((rows, d), F32).at[:b].set(c)
    out = pl.pallas_call(
        _ada_kernel,
        grid=(depth, nm // ADA_COLS),
        in_specs=[
            pl.BlockSpec((rows, d), lambda l, n: (0, 0)),
            pl.BlockSpec((1, d, ADA_COLS), lambda l, n: (l, 0, n)),
            pl.BlockSpec((1, 1, ADA_COLS), lambda l, n: (l, 0, n)),
        ],
        out_specs=pl.BlockSpec((1, rows, ADA_COLS), lambda l, n: (l, 0, n)),
        out_shape=jax.ShapeDtypeStruct((depth, rows, nm), F32),
        compiler_params=_cparams(2),
        name="ada_mod",
    )(c_pad, ada_w, ada_b.reshape(depth, 1, nm))
    return out[:, :b, :].reshape(depth, b, N_MOD, 1, d)


def _conv_kernel(tiles_per_seq, x_ref, mod_ref, g_ref, win_ref, cw_ref, wout_ref,
                 o_ref, zbuf):
    tm, d = x_ref.shape
    t = pl.program_id(0)

    @pl.when(t % tiles_per_seq == 0)
    def _():
        zbuf[0:SUBLANES, :] = jnp.zeros((SUBLANES, d), F32)

    x = x_ref[...]
    h = _norm_mod(x, g_ref[...], mod_ref[0], mod_ref[1]).astype(BF16)
    cx = jnp.dot(h, win_ref[:, d:], preferred_element_type=F32)
    z = cx[:, :d] * cx[:, d:]
    zbuf[SUBLANES:SUBLANES + tm, :] = z
    gate = jnp.dot(h, win_ref[:, :d], preferred_element_type=F32)
    z1 = zbuf[SUBLANES - 1:SUBLANES - 1 + tm, :]
    z2 = zbuf[SUBLANES - 2:SUBLANES - 2 + tm, :]
    zc = cw_ref[0] * z2 + cw_ref[1] * z1 + cw_ref[2] * z
    zbuf[0:SUBLANES, :] = z[tm - SUBLANES:, :]
    gated = (gate * zc).astype(BF16)
    y = jnp.dot(gated, wout_ref[...], preferred_element_type=F32)
    o_ref[...] = x + mod_ref[2] * y


def _conv_call(x, mod_l, g, w_in, conv_w, w_out, seq, cast_jobs=()):
    n, d = x.shape
    tm = TOKEN_TILE
    tps = seq // tm
    steps = n // tm
    job_in, job_out, job_shapes, job_args = _cast_job_specs(cast_jobs, steps)
    in_specs = [
        pl.BlockSpec((tm, d), lambda t: (t, 0)),
        pl.BlockSpec((None, N_MOD, 1, d), lambda t: (t // tps, 0, 0, 0)),
        _resident((1, d), lambda t: (0, 0)),
        _layer_spec(w_in),
        _resident((CONV_W, 1, d), lambda t: (0, 0, 0)),
        _layer_spec(w_out),
    ]
    outs = pl.pallas_call(
        _with_cast_jobs(functools.partial(_conv_kernel, tps), len(in_specs), 1, len(cast_jobs)),
        grid=(steps,),
        in_specs=in_specs + job_in,
        out_specs=[pl.BlockSpec((tm, d), lambda t: (t, 0))] + job_out,
        out_shape=[jax.ShapeDtypeStruct((n, d), F32)] + job_shapes,
        scratch_shapes=[pltpu.VMEM((tm + SUBLANES, d), F32)],
        compiler_params=_cparams(1),
        name="conv_mixer",
    )(x, mod_l, g.reshape(1, d), w_in, conv_w.reshape(CONV_W, 1, d), w_out, *job_args)
    return outs[0], list(outs[1:])


def _mlp_residual(x1, mod_ref, g, w1_ref, w2_ref):
    d, ff = w1_ref.shape
    h = _norm_mod(x1, g, mod_ref[3], mod_ref[4]).astype(BF16)
    n_chunks = ff // FF_CHUNK

    def up(c):
        return jnp.dot(h, w1_ref[:, c * FF_CHUNK:(c + 1) * FF_CHUNK], preferred_element_type=F32)

    acc = jnp.zeros(x1.shape, F32)
    pre = up(0)
    for c in range(n_chunks):
        nxt = up(c + 1) if c + 1 < n_chunks else None
        a = jnp.maximum(pre, 0.0)
        acc = acc + jnp.dot((a * a).astype(BF16), w2_ref[c * FF_CHUNK:(c + 1) * FF_CHUNK, :],
                            preferred_element_type=F32)
        pre = nxt
    return x1 + mod_ref[5] * acc


def _mlp_kernel(x_ref, mod_ref, g_ref, w1_ref, w2_ref, o_ref):
    o_ref[...] = _mlp_residual(x_ref[...], mod_ref, g_ref[...], w1_ref, w2_ref)


def _omlp_kernel(x_ref, aT_ref, mod_ref, g_ref, ow_ref, w1_ref, w2_ref, o_ref):
    y = lax.dot_general(aT_ref[0], ow_ref[...], TN_DIMS, preferred_element_type=F32)
    x1 = x_ref[...] + mod_ref[2] * y
    o_ref[...] = _mlp_residual(x1, mod_ref, g_ref[...], w1_ref, w2_ref)


def _mlp_call(x, mod_l, g, w1, w2, seq, attn=None, o_w=None, o_layer=None, cast_jobs=()):
    n, d = x.shape
    tm = TOKEN_TILE
    tps = seq // tm
    steps = n // tm
    job_in, job_out, job_shapes, job_args = _cast_job_specs(cast_jobs, steps)
    tile = pl.BlockSpec((tm, d), lambda t: (t, 0))
    mod_spec = pl.BlockSpec((None, N_MOD, 1, d), lambda t: (t // tps, 0, 0, 0))
    g_spec = _resident((1, d), lambda t: (0, 0))
    w_specs = [_layer_spec(w1), _layer_spec(w2)]
    if attn is None:
        body, name = _mlp_kernel, "mlp"
        in_specs = [tile, mod_spec, g_spec] + w_specs
        args = (x, mod_l, g.reshape(1, d), w1, w2)
    else:
        body, name = _omlp_kernel, "oproj_mlp"
        attn_spec = pl.BlockSpec((1, d, tm), lambda t: (t // tps, 0, t % tps))
        in_specs = [tile, attn_spec, mod_spec, g_spec, _layer_spec(o_w, o_layer)] + w_specs
        args = (x, attn, mod_l, g.reshape(1, d), o_w, w1, w2)
    outs = pl.pallas_call(
        _with_cast_jobs(body, len(in_specs), 1, len(cast_jobs)),
        grid=(steps,),
        in_specs=in_specs + job_in,
        out_specs=[tile] + job_out,
        out_shape=[jax.ShapeDtypeStruct((n, d), F32)] + job_shapes,
        compiler_params=_cparams(1),
        name=name,
    )(*args, *job_args)
    return outs[0], list(outs[1:])


def _extra_base(head):
    return HEAD_DIM if head % 2 == 0 else 0


ONE_LANE = SLAB - 1


EXTRA_ROWS = 16
Q_ROWS = HEAD_DIM + EXTRA_ROWS


def _selection_matrices():
    sel_k = np.zeros((SLAB, N_HEADS // 2 * SLAB), np.float32)
    sel_q = np.zeros((N_HEADS * EXTRA_ROWS, SLAB), np.float32)
    for h in range(N_HEADS):
        base = (h // 2) * SLAB + _extra_base(h)
        for piece in range(3):
            sel_q[h * EXTRA_ROWS + piece, piece * N_HEADS + h] = 1.0
            sel_q[h * EXTRA_ROWS + 3 + piece, ONE_LANE] = 1.0
            sel_k[ONE_LANE, base + piece] = 1.0
            sel_k[piece * N_HEADS + h, base + 3 + piece] = -1.0
    return jnp.asarray(sel_k, BF16), jnp.asarray(sel_q, BF16)


def _pieces(f, index):
    hi, mid, lo = _split3(f)
    one = jnp.where(index == ONE_LANE, 1.0, 0.0)
    return jnp.where(index < N_HEADS, hi,
                     jnp.where(index < 2 * N_HEADS, mid,
                               jnp.where(index < 3 * N_HEADS, lo, one))).astype(BF16)


def _kv_kernel(tiles_per_seq, x_ref, g_ref, wk_ref, wvT_ref, wf_ref, bf_ref, kg_ref,
               selk_ref, tri_ref, k_ref, vT_ref, fT_ref, carry):
    tm, d = x_ref.shape
    t = pl.program_id(0)

    @pl.when(t % tiles_per_seq == 0)
    def _():
        carry[...] = jnp.zeros(carry.shape, F32)

    h = (_rms_scale(x_ref[...]) * g_ref[...]).astype(BF16)

    xf = jnp.dot(h, wf_ref[...], preferred_element_type=F32) + bf_ref[...]
    k = jnp.dot(h, wk_ref[...], preferred_element_type=F32)
    log_f = jnp.minimum(xf, 0.0) - jnp.log1p(jnp.exp(-jnp.abs(xf)))
    pieces = jnp.concatenate([p.astype(BF16) for p in _split3(log_f)], axis=-1)
    sums = jnp.dot(tri_ref[...], pieces, preferred_element_type=F32)
    vT = lax.dot_general(wvT_ref[...], h, NT_DIMS, preferred_element_type=F32)
    f = carry[0:1, :] + sums[:, 0:SLAB] + sums[:, SLAB:2 * SLAB] + sums[:, 2 * SLAB:]
    carry[...] = jnp.broadcast_to(f[tm - 1:tm, :], carry.shape)
    f2 = f * LOG2E
    fT_ref[0] = f2.T

    lane = lax.broadcasted_iota(jnp.int32, (tm, SLAB), 1)
    extras = jnp.dot(_pieces(f2, lane), selk_ref[...], preferred_element_type=F32)

    low = lane < HEAD_DIM
    for j in range(N_HEADS // 2):
        ks = k[:, j * SLAB:(j + 1) * SLAB]
        k2 = ks * ks
        ms_lo = jnp.sum(jnp.where(low, k2, 0.0), axis=-1, keepdims=True) * (1.0 / HEAD_DIM)
        ms_hi = jnp.sum(jnp.where(low, 0.0, k2), axis=-1, keepdims=True) * (1.0 / HEAD_DIM)
        kn = ks * jnp.where(low, lax.rsqrt(ms_lo + EPS), lax.rsqrt(ms_hi + EPS)) * kg_ref[...]
        ex = extras[:, j * SLAB:(j + 1) * SLAB]
        k_ref[0, 2 * j] = jnp.where(low, kn, ex).astype(BF16)
        k_ref[0, 2 * j + 1] = jnp.where(low, ex, kn).astype(BF16)

    row = lax.broadcasted_iota(jnp.int32, (V_ROWS - HEAD_DIM, tm), 0)
    ones_blk = jnp.where(row == 0, 1.0, 0.0).astype(BF16)
    for hd in range(N_HEADS):
        vT_ref[0, hd, 0, 0:HEAD_DIM, :] = vT[hd * HEAD_DIM:(hd + 1) * HEAD_DIM, :].astype(BF16)
        vT_ref[0, hd, 0, HEAD_DIM:V_ROWS, :] = ones_blk


def _kv_call(x, g, w_k, w_vT, w_f3, b_f3, k_g2, sel_k, batch, seq):
    n, d = x.shape
    tm = TOKEN_TILE
    tps = seq // tm
    tpk = KV_TILE // tm
    tri = jnp.asarray(np.tril(np.ones((tm, tm), np.float32)), BF16)
    return pl.pallas_call(
        functools.partial(_kv_kernel, tps),
        grid=(n // tm,),
        in_specs=[
            pl.BlockSpec((tm, d), lambda t: (t, 0)),
            _resident((1, d), lambda t: (0, 0)),
            _resident((d, d), lambda t: (0, 0)),
            _resident((d, d), lambda t: (0, 0)),
            _resident((d, SLAB), lambda t: (0, 0)),
            _resident((1, SLAB), lambda t: (0, 0)),
            _resident((1, SLAB), lambda t: (0, 0)),
            _resident((SLAB, N_HEADS // 2 * SLAB), lambda t: (0, 0)),
            _resident((tm, tm), lambda t: (0, 0)),
        ],
        out_specs=[
            pl.BlockSpec((1, N_HEADS, tm, SLAB), lambda t: (t // tps, 0, t % tps, 0)),
            pl.BlockSpec((1, N_HEADS, 1, V_ROWS, tm),
                         lambda t: (t // tps, 0, (t % tps) // tpk, 0, t % tpk)),
            pl.BlockSpec((1, SLAB, tm), lambda t: (t // tps, 0, t % tps)),
        ],
        out_shape=[
            jax.ShapeDtypeStruct((batch, N_HEADS, seq, SLAB), BF16),
            jax.ShapeDtypeStruct((batch, N_HEADS, seq // KV_TILE, V_ROWS, KV_TILE), BF16),
            jax.ShapeDtypeStruct((batch, SLAB, seq), F32),
        ],
        scratch_shapes=[pltpu.VMEM((SUBLANES, SLAB), F32)],
        compiler_params=_cparams(1),
        name="shared_kv",
    )(x, g.reshape(1, d), w_k, w_vT, w_f3, b_f3, k_g2, sel_k, tri)


def _q_kernel(shift_ref, x_ref, mod_ref, g_ref, qwT_ref, qg_ref, fT_ref, selq_ref, o_ref):
    tm, d = x_ref.shape
    row = lax.broadcasted_iota(jnp.int32, (SLAB, tm), 0)
    extras = jnp.dot(selq_ref[...], _pieces(fT_ref[0] - shift_ref[0], row),
                     preferred_element_type=F32)
    h = _norm_mod(x_ref[...], g_ref[...], mod_ref[0], mod_ref[1]).astype(BF16)
    qT = lax.dot_general(qwT_ref[...], h, NT_DIMS, preferred_element_type=F32)
    for hd in range(N_HEADS):
        qh = qT[hd * HEAD_DIM:(hd + 1) * HEAD_DIM, :]
        ms = jnp.mean(qh * qh, axis=0, keepdims=True)
        qn = qh * lax.rsqrt(ms + EPS) * qg_ref[...]
        o_ref[0, hd, 0:HEAD_DIM, :] = qn.astype(BF16)
        o_ref[0, hd, HEAD_DIM:Q_ROWS, :] = (
            extras[hd * EXTRA_ROWS:(hd + 1) * EXTRA_ROWS, :].astype(BF16))


def _q_call(shift, x, mod_l, g, q_wT, layer, q_g_cols, fT, sel_q, batch, seq):
    n, d = x.shape
    tm = Q_TOKEN_TILE
    tps = seq // tm
    return pl.pallas_call(
        _q_kernel,
        grid=(n // tm,),
        in_specs=[
            pl.BlockSpec(memory_space=pltpu.SMEM),
            pl.BlockSpec((tm, d), lambda t: (t, 0)),
            pl.BlockSpec((None, N_MOD, 1, d), lambda t: (t // tps, 0, 0, 0)),
            _resident((1, d), lambda t: (0, 0)),
            _layer_spec(q_wT, layer),
            _resident((HEAD_DIM, tm), lambda t: (0, 0)),
            pl.BlockSpec((1, SLAB, tm), lambda t: (t // tps, 0, t % tps)),
            _resident((N_HEADS * EXTRA_ROWS, SLAB), lambda t: (0, 0)),
        ],
        out_specs=pl.BlockSpec((1, N_HEADS, Q_ROWS, tm), lambda t: (t // tps, 0, 0, t % tps)),
        out_shape=jax.ShapeDtypeStruct((batch, N_HEADS, Q_ROWS, seq), BF16),
        compiler_params=_cparams(1),
        name="fox_query",
    )(shift, x, mod_l, g.reshape(1, d), q_wT, q_g_cols, fT, sel_q)


def _store_normalized(acc_ref, o_ref):
    for hh in range(2):
        inv = 1.0 / acc_ref[hh, HEAD_DIM:HEAD_DIM + 1, :]
        o_ref[0, hh * HEAD_DIM:(hh + 1) * HEAD_DIM, :] = (
            acc_ref[hh, 0:HEAD_DIM, :] * inv).astype(BF16)


def _expand_queries(qc_ref, q_ref):
    tq = q_ref.shape[-1]
    zeros = jnp.zeros((HEAD_DIM - EXTRA_ROWS, tq), BF16)
    q_ref[0, 0:Q_ROWS, :] = qc_ref[0, 0]
    q_ref[0, Q_ROWS:SLAB, :] = zeros
    q_ref[1, 0:EXTRA_ROWS, :] = qc_ref[0, 1, HEAD_DIM:Q_ROWS, :]
    q_ref[1, EXTRA_ROWS:HEAD_DIM, :] = zeros
    q_ref[1, HEAD_DIM:SLAB, :] = qc_ref[0, 1, 0:HEAD_DIM, :]


def _attn_kernel(qc_ref, k_ref, vT_ref, o_ref, acc_ref, m_ref, qT_ref):
    tq = qT_ref.shape[-1]
    tk = vT_ref.shape[-1]
    i = pl.program_id(2)
    n_kv = (i * tq + tq + tk - 1) // tk

    _expand_queries(qc_ref, qT_ref)
    for hh in range(2):
        acc_ref[hh] = jnp.zeros((V_ROWS, tq), F32)
        m_ref[hh] = jnp.full((SUBLANES, tq), MASK_VALUE, F32)

    def step(kb, masked):
        start = pl.multiple_of(kb * tk, tk)
        for hh in range(2):
            s = jnp.dot(k_ref[0, hh, pl.ds(start, tk), :], qT_ref[hh],
                        preferred_element_type=F32)
            if masked:
                key_pos = kb * tk + lax.broadcasted_iota(jnp.int32, (tk, tq), 0)
                q_pos = i * tq + lax.broadcasted_iota(jnp.int32, (tk, tq), 1)
                s = jnp.where(key_pos <= q_pos, s, MASK_VALUE)
            m_old = m_ref[hh, 0:1, :]
            m_new = jnp.maximum(m_old, jnp.max(s, axis=0, keepdims=True))
            alpha = jnp.exp2(m_old - m_new)
            p = jnp.exp2(s - m_new).astype(BF16)
            pv = jnp.dot(vT_ref[0, hh, kb], p, preferred_element_type=F32)
            acc_ref[hh] = alpha * acc_ref[hh] + pv
            m_ref[hh] = jnp.broadcast_to(m_new, (SUBLANES, tq))

    def body(kb, carry):
        step(kb, False)
        return carry

    lax.fori_loop(0, n_kv - 1, body, 0)
    step(n_kv - 1, True)

    _store_normalized(acc_ref, o_ref)


def _attn_preshifted_kernel(qc_ref, k_ref, vT_ref, o_ref, acc_ref, p_ref, qT_ref):
    tq = qT_ref.shape[-1]
    tk = vT_ref.shape[-1]
    assert tq == tk
    i = pl.program_id(2)

    def probabilities(kb, slot):
        start = pl.multiple_of(kb * tk, tk)
        for hh in range(2):
            s = jnp.dot(k_ref[0, hh, pl.ds(start, tk), :], qT_ref[hh],
                        preferred_element_type=F32)
            p_ref[slot, hh] = jnp.exp2(s).astype(BF16)

    def diagonal_probabilities(slot):
        causal = (lax.broadcasted_iota(jnp.int32, (DIAG_BAND, tq), 0)
                  <= lax.broadcasted_iota(jnp.int32, (DIAG_BAND, tq), 1))
        for hh in range(2):
            q = qT_ref[hh]
            for r in range(tk // DIAG_BAND):
                lo = r * DIAG_BAND
                start = pl.multiple_of(i * tk + lo, DIAG_BAND)
                s = jnp.dot(k_ref[0, hh, pl.ds(start, DIAG_BAND), :], q[:, lo:],
                            preferred_element_type=F32)
                p_ref[slot, hh, lo:lo + DIAG_BAND, lo:] = jnp.exp2(
                    jnp.where(causal[:, :tq - lo], s, MASK_VALUE)).astype(BF16)
                if lo:
                    p_ref[slot, hh, lo:lo + DIAG_BAND, 0:lo] = jnp.zeros((DIAG_BAND, lo), BF16)

    def accumulate(kb, slot):
        for hh in range(2):
            acc_ref[hh] += jnp.dot(vT_ref[0, hh, kb], p_ref[slot, hh],
                                   preferred_element_type=F32)

    _expand_queries(qc_ref, qT_ref)
    for hh in range(2):
        acc_ref[hh] = jnp.zeros((V_ROWS, tq), F32)
    diagonal_probabilities(0)
    odd = i % 2

    @pl.when(odd == 1)
    def _():
        accumulate(i, 0)
        probabilities(0, 0)

    def body(t, pending):
        kb = odd + 2 * t
        accumulate(pending, 0)
        probabilities(kb, 1)
        probabilities(kb + 1, 0)
        accumulate(kb, 1)
        return kb + 1

    accumulate(lax.fori_loop(0, i // 2, body, jnp.where(odd == 1, 0, i)), 0)

    _store_normalized(acc_ref, o_ref)


def _attn_call(qT, k, vT, batch, seq, preshifted):
    nkb = seq // KV_TILE
    pairs = N_HEADS // 2
    if preshifted:
        body = _attn_preshifted_kernel
        scratch = [pltpu.VMEM((2, V_ROWS, Q_TILE), F32), pltpu.VMEM((2, 2, KV_TILE, Q_TILE), BF16)]
    else:
        body = _attn_kernel
        scratch = [pltpu.VMEM((2, V_ROWS, Q_TILE), F32), pltpu.VMEM((2, SUBLANES, Q_TILE), F32)]
    scratch.append(pltpu.VMEM((2, SLAB, Q_TILE), BF16))
    return pl.pallas_call(
        body,
        grid=(batch, pairs, seq // Q_TILE),
        in_specs=[
            pl.BlockSpec((1, 2, Q_ROWS, Q_TILE), lambda b, j, i: (b, j, 0, i)),
            pl.BlockSpec((1, 2, seq, SLAB), lambda b, j, i: (b, j, 0, 0)),
            pl.BlockSpec((1, 2, nkb, V_ROWS, KV_TILE), lambda b, j, i: (b, j, 0, 0, 0)),
        ],
        out_specs=pl.BlockSpec((1, SLAB, Q_TILE), lambda b, j, i: (b, j, i)),
        out_shape=jax.ShapeDtypeStruct((batch, N_HEADS * HEAD_DIM, seq), BF16),
        scratch_shapes=scratch,
        compiler_params=_cparams(3),
        name="fox_attention_preshifted" if preshifted else "fox_attention",
    )(qT, k, vT)


def kernel(x, c, ada_w, ada_b, norm_mix_g, norm_mlp_g, sc_w_in, sc_conv, sc_w_out,
           kv_norm_g, w_kv, k_norm_g, w_f, b_f, q_w, q_norm_g, o_w, mlp_w1, mlp_w2):
    batch, seq, d = x.shape
    depth = ada_w.shape[0]
    n_conv = sc_w_in.shape[0]
    assert d == N_HEADS * HEAD_DIM and seq % KV_TILE == 0 and seq % Q_TOKEN_TILE == 0
    assert KV_TILE % TOKEN_TILE == 0 and KV_TILE == Q_TILE

    mod = _ada_call(c, ada_w, ada_b)
    sel_k, sel_q = _selection_matrices()
    q_scale = LOG2E / np.sqrt(HEAD_DIM)

    q_wT_all, o_w_all = jnp.swapaxes(q_w, 1, 2).astype(BF16), o_w.astype(BF16)
    conv_w_bf16 = {0: [sc_w_in[0].astype(BF16), sc_w_out[0].astype(BF16)]} if n_conv else {}
    mlp_w_bf16 = {} if n_conv else {0: [mlp_w1[0].astype(BF16), mlp_w2[0].astype(BF16)]}

    def jobs_for_next(l):
        if l + 1 >= depth:
            return []
        return [(sc_w_in, l + 1), (sc_w_out, l + 1)] if l + 1 < n_conv \
            else [(mlp_w1, l + 1), (mlp_w2, l + 1)]

    def keep_casts(l, casts):
        if casts:
            (conv_w_bf16 if l + 1 < n_conv else mlp_w_bf16)[l + 1] = casts

    xs = x.reshape(batch * seq, d)
    k = vT = fT = None
    for l in range(depth):
        if l < n_conv:
            xs, mlp_w_bf16[l] = _conv_call(xs, mod[l], norm_mix_g[l], conv_w_bf16[l][0], sc_conv[l],
                                           conv_w_bf16[l][1], seq,
                                           cast_jobs=[(mlp_w1, l), (mlp_w2, l)])
            xs, casts = _mlp_call(xs, mod[l], norm_mlp_g[l], *mlp_w_bf16[l], seq,
                                  cast_jobs=jobs_for_next(l))
            keep_casts(l, casts)
        else:
            i = l - n_conv
            q_g_cols = jnp.broadcast_to((q_norm_g[i] * q_scale)[:, None], (HEAD_DIM, Q_TOKEN_TILE))
            bound = (np.sqrt(HEAD_DIM) * BOUND_SLACK) * jnp.max(jnp.abs(q_norm_g[i])) \
                * jnp.max(jnp.abs(k_norm_g))
            preshift_ok = bound <= MAX_PRESHIFT
            shift = jnp.where(preshift_ok, bound * LOG2E, 0.0).reshape(1).astype(F32)
            qT = _q_call(shift, xs, mod[l], norm_mix_g[l], q_wT_all, i, q_g_cols, fT,
                         sel_q, batch, seq)
            attn = lax.cond(
                preshift_ok,
                functools.partial(_attn_call, batch=batch, seq=seq, preshifted=True),
                functools.partial(_attn_call, batch=batch, seq=seq, preshifted=False),
                qT, k, vT)
            xs, casts = _mlp_call(xs, mod[l], norm_mlp_g[l], *mlp_w_bf16[l], seq,
                                  attn=attn, o_w=o_w_all, o_layer=i, cast_jobs=jobs_for_next(l))
            keep_casts(l, casts)
        if l == n_conv - 1:
            w_f3 = jnp.zeros((d, SLAB), F32).at[:, :3 * N_HEADS].set(jnp.tile(w_f, (1, 3)))
            b_f3 = jnp.zeros((1, SLAB), F32).at[0, :3 * N_HEADS].set(jnp.tile(b_f, 3))
            k, vT, fT = _kv_call(xs, kv_norm_g, w_kv[:, :d].astype(BF16),
                                 w_kv[:, d:].T.astype(BF16), w_f3.astype(BF16), b_f3,
                                 jnp.tile(k_norm_g, 2).reshape(1, SLAB), sel_k, batch, seq)
    return xs.reshape(batch, seq, d)
```

```python
import functools

import numpy as np
import jax
import jax.numpy as jnp
from jax import lax
from jax.experimental import pallas as pl
from jax.experimental.pallas import tpu as pltpu

F32 = jnp.float32
BF16 = jnp.bfloat16

SUBLANES = 8
N_HEADS = 16
HEAD_DIM = 64
SLAB = 2 * HEAD_DIM
V_ROWS = 2 * HEAD_DIM
N_MOD = 6
CONV_W = 3
EPS = 1e-6
MASK_VALUE = -1e30

TOKEN_TILE = 512
Q_TOKEN_TILE = 1024
FF_CHUNK = 1024
Q_TILE = 1024
KV_TILE = 1024
DIAG_BAND = 256
ADA_COLS = 1536
LOG2E = 1.4426950408889634
MAX_PRESHIFT = 40.0
BOUND_SLACK = 1.0 + 2.0 ** -6
VMEM_LIMIT_BYTES = 56 * 1024 * 1024

NT_DIMS = (((1,), (1,)), ((), ()))
TN_DIMS = (((0,), (0,)), ((), ()))


def _cparams(n_axes):
    return pltpu.CompilerParams(
        dimension_semantics=("arbitrary",) * n_axes,
        vmem_limit_bytes=VMEM_LIMIT_BYTES)


def _resident(block_shape, index_map):
    return pl.BlockSpec(block_shape, index_map, pipeline_mode=pl.Buffered(1))


def _layer_spec(w, layer=None):
    if w.ndim == 2:
        return _resident(w.shape, lambda *_: (0, 0))
    _, rows, cols = w.shape
    return _resident((None, rows, cols), lambda *_: (layer, 0, 0))


def _with_cast_jobs(body, n_in, n_out, n_jobs):
    def kernel(*refs):
        ins, rest = refs[:n_in], refs[n_in:]
        job_src, rest = rest[:n_jobs], rest[n_jobs:]
        outs, rest = rest[:n_out], rest[n_out:]
        job_dst, scratch = rest[:n_jobs], rest[n_jobs:]
        body(*ins, *outs, *scratch)
        for src, dst in zip(job_src, job_dst):
            dst[...] = src[...].astype(BF16)
    return kernel


def _cast_job_specs(jobs, n_steps):
    in_specs, out_specs, out_shapes, operands = [], [], [], []
    for stack, layer in jobs:
        _, rows, cols = stack.shape
        slab = rows // n_steps
        assert slab * n_steps == rows and slab % 16 == 0
        in_specs.append(pl.BlockSpec((None, slab, cols), lambda t, layer=layer: (layer, t, 0)))
        out_specs.append(pl.BlockSpec((slab, cols), lambda t: (t, 0)))
        out_shapes.append(jax.ShapeDtypeStruct((rows, cols), BF16))
        operands.append(stack)
    return in_specs, out_specs, out_shapes, operands


def _rms_scale(x):
    ms = jnp.mean(x * x, axis=-1, keepdims=True)
    return x * lax.rsqrt(ms + EPS)


def _norm_mod(x, g, shift, scale):
    return (_rms_scale(x) * g) * (1.0 + scale) + shift


def _split3(v):
    hi = v.astype(BF16).astype(F32)
    r = v - hi
    mid = r.astype(BF16).astype(F32)
    lo = (r - mid).astype(BF16).astype(F32)
    return hi, mid, lo


def _ada_kernel(c_ref, w_ref, b_ref, o_ref):
    c = c_ref[...]
    ca = c * jax.nn.sigmoid(c)
    o_ref[0] = jnp.dot(ca.astype(BF16), w_ref[0].astype(BF16),
                       preferred_element_type=F32) + b_ref[0]


def _ada_call(c, ada_w, ada_b):
    depth, d, nm = ada_w.shape
    b = c.shape[0]
    rows = SUBLANES
    c_pad = jnp.zeros((rows, d), F32).at[:b].set(c)
    out = pl.pallas_call(
        _ada_kernel,
        grid=(depth, nm // ADA_COLS),
        in_specs=[
            pl.BlockSpec((rows, d), lambda l, n: (0, 0)),
            pl.BlockSpec((1, d, ADA_COLS), lambda l, n: (l, 0, n)),
            pl.BlockSpec((1, 1, ADA_COLS), lambda l, n: (l, 0, n)),
        ],
        out_specs=pl.BlockSpec((1, rows, ADA_COLS), lambda l, n: (l, 0, n)),
        out_shape=jax.ShapeDtypeStruct((depth, rows, nm), F32),
        compiler_params=_cparams(2),
        name="ada_mod",
    )(c_pad, ada_w, ada_b.reshape(depth, 1, nm))
    return out[:, :b, :].reshape(depth, b, N_MOD, 1, d)


def _conv_kernel(tiles_per_seq, x_ref, mod_ref, g_ref, win_ref, cw_ref, wout_ref,
                 o_ref, zbuf):
    tm, d = x_ref.shape
    t = pl.program_id(0)

    @pl.when(t % tiles_per_seq == 0)
    def _():
        zbuf[0:SUBLANES, :] = jnp.zeros((SUBLANES, d), F32)

    x = x_ref[...]
    h = _norm_mod(x, g_ref[...], mod_ref[0], mod_ref[1]).astype(BF16)
    cx = jnp.dot(h, win_ref[:, d:], preferred_element_type=F32)
    z = cx[:, :d] * cx[:, d:]
    zbuf[SUBLANES:SUBLANES + tm, :] = z
    gate = jnp.dot(h, win_ref[:, :d], preferred_element_type=F32)
    z1 = zbuf[SUBLANES - 1:SUBLANES - 1 + tm, :]
    z2 = zbuf[SUBLANES - 2:SUBLANES - 2 + tm, :]
    zc = cw_ref[0] * z2 + cw_ref[1] * z1 + cw_ref[2] * z
    zbuf[0:SUBLANES, :] = z[tm - SUBLANES:, :]
    gated = (gate * zc).astype(BF16)
    y = jnp.dot(gated, wout_ref[...], preferred_element_type=F32)
    o_ref[...] = x + mod_ref[2] * y


def _conv_call(x, mod_l, g, w_in, conv_w, w_out, seq, cast_jobs=()):
    n, d = x.shape
    tm = TOKEN_TILE
    tps = seq // tm
    steps = n // tm
    job_in, job_out, job_shapes, job_args = _cast_job_specs(cast_jobs, steps)
    in_specs = [
        pl.BlockSpec((tm, d), lambda t: (t, 0)),
        pl.BlockSpec((None, N_MOD, 1, d), lambda t: (t // tps, 0, 0, 0)),
        _resident((1, d), lambda t: (0, 0)),
        _layer_spec(w_in),
        _resident((CONV_W, 1, d), lambda t: (0, 0, 0)),
        _layer_spec(w_out),
    ]
    outs = pl.pallas_call(
        _with_cast_jobs(functools.partial(_conv_kernel, tps), len(in_specs), 1, len(cast_jobs)),
        grid=(steps,),
        in_specs=in_specs + job_in,
        out_specs=[pl.BlockSpec((tm, d), lambda t: (t, 0))] + job_out,
        out_shape=[jax.ShapeDtypeStruct((n, d), F32)] + job_shapes,
        scratch_shapes=[pltpu.VMEM((tm + SUBLANES, d), F32)],
        compiler_params=_cparams(1),
        name="conv_mixer",
    )(x, mod_l, g.reshape(1, d), w_in, conv_w.reshape(CONV_W, 1, d), w_out, *job_args)
    return outs[0], list(outs[1:])


def _mlp_residual(x1, mod_ref, g, w1_ref, w2_ref):
    d, ff = w1_ref.shape
    h = _norm_mod(x1, g, mod_ref[3], mod_ref[4]).astype(BF16)
    n_chunks = ff // FF_CHUNK

    def up(c):
        return jnp.dot(h, w1_ref[:, c * FF_CHUNK:(c + 1) * FF_CHUNK], preferred_element_type=F32)

    acc = jnp.zeros(x1.shape, F32)
    pre = up(0)
    for c in range(n_chunks):
        nxt = up(c + 1) if c + 1 < n_chunks else None
        a = jnp.maximum(pre, 0.0)
        acc = acc + jnp.dot((a * a).astype(BF16), w2_ref[c * FF_CHUNK:(c + 1) * FF_CHUNK, :],
                            preferred_element_type=F32)
        pre = nxt
    return x1 + mod_ref[5] * acc


def _mlp_kernel(x_ref, mod_ref, g_ref, w1_ref, w2_ref, o_ref):
    o_ref[...] = _mlp_residual(x_ref[...], mod_ref, g_ref[...], w1_ref, w2_ref)


def _omlp_kernel(x_ref, aT_ref, mod_ref, g_ref, ow_ref, w1_ref, w2_ref, o_ref):
    y = lax.dot_general(aT_ref[0], ow_ref[...], TN_DIMS, preferred_element_type=F32)
    x1 = x_ref[...] + mod_ref[2] * y
    o_ref[...] = _mlp_residual(x1, mod_ref, g_ref[...], w1_ref, w2_ref)


def _mlp_call(x, mod_l, g, w1, w2, seq, attn=None, o_w=None, o_layer=None, cast_jobs=()):
    n, d = x.shape
    tm = TOKEN_TILE
    tps = seq // tm
    steps = n // tm
    job_in, job_out, job_shapes, job_args = _cast_job_specs(cast_jobs, steps)
    tile = pl.BlockSpec((tm, d), lambda t: (t, 0))
    mod_spec = pl.BlockSpec((None, N_MOD, 1, d), lambda t: (t // tps, 0, 0, 0))
    g_spec = _resident((1, d), lambda t: (0, 0))
    w_specs = [_layer_spec(w1), _layer_spec(w2)]
    if attn is None:
        body, name = _mlp_kernel, "mlp"
        in_specs = [tile, mod_spec, g_spec] + w_specs
        args = (x, mod_l, g.reshape(1, d), w1, w2)
    else:
        body, name = _omlp_kernel, "oproj_mlp"
        attn_spec = pl.BlockSpec((1, d, tm), lambda t: (t // tps, 0, t % tps))
        in_specs = [tile, attn_spec, mod_spec, g_spec, _layer_spec(o_w, o_layer)] + w_specs
        args = (x, attn, mod_l, g.reshape(1, d), o_w, w1, w2)
    outs = pl.pallas_call(
        _with_cast_jobs(body, len(in_specs), 1, len(cast_jobs)),
        grid=(steps,),
        in_specs=in_specs + job_in,
        out_specs=[tile] + job_out,
        out_shape=[jax.ShapeDtypeStruct((n, d), F32)] + job_shapes,
        compiler_params=_cparams(1),
        name=name,
    )(*args, *job_args)
    return outs[0], list(outs[1:])


def _extra_base(head):
    return HEAD_DIM if head % 2 == 0 else 0


ONE_LANE = SLAB - 1


EXTRA_ROWS = 16
Q_ROWS = HEAD_DIM + EXTRA_ROWS


def _selection_matrices():
    sel_k = np.zeros((SLAB, N_HEADS // 2 * SLAB), np.float32)
    sel_q = np.zeros((N_HEADS * EXTRA_ROWS, SLAB), np.float32)
    for h in range(N_HEADS):
        base = (h // 2) * SLAB + _extra_base(h)
        for piece in range(3):
            sel_q[h * EXTRA_ROWS + piece, piece * N_HEADS + h] = 1.0
            sel_q[h * EXTRA_ROWS + 3 + piece, ONE_LANE] = 1.0
            sel_k[ONE_LANE, base + piece] = 1.0
            sel_k[piece * N_HEADS + h, base + 3 + piece] = -1.0
    return jnp.asarray(sel_k, BF16), jnp.asarray(sel_q, BF16)


def _pieces(f, index):
    hi, mid, lo = _split3(f)
    one = jnp.where(index == ONE_LANE, 1.0, 0.0)
    return jnp.where(index < N_HEADS, hi,
                     jnp.where(index < 2 * N_HEADS, mid,
                               jnp.where(index < 3 * N_HEADS, lo, one))).astype(BF16)


def _kv_kernel(tiles_per_seq, x_ref, g_ref, wk_ref, wvT_ref, wf_ref, bf_ref, kg_ref,
               selk_ref, tri_ref, k_ref, vT_ref, fT_ref, carry):
    tm, d = x_ref.shape
    t = pl.program_id(0)

    @pl.when(t % tiles_per_seq == 0)
    def _():
        carry[...] = jnp.zeros(carry.shape, F32)

    h = (_rms_scale(x_ref[...]) * g_ref[...]).astype(BF16)

    xf = jnp.dot(h, wf_ref[...], preferred_element_type=F32) + bf_ref[...]
    k = jnp.dot(h, wk_ref[...], preferred_element_type=F32)
    log_f = jnp.minimum(xf, 0.0) - jnp.log1p(jnp.exp(-jnp.abs(xf)))
    pieces = jnp.concatenate([p.astype(BF16) for p in _split3(log_f)], axis=-1)
    sums = jnp.dot(tri_ref[...], pieces, preferred_element_type=F32)
    vT = lax.dot_general(wvT_ref[...], h, NT_DIMS, preferred_element_type=F32)
    f = carry[0:1, :] + sums[:, 0:SLAB] + sums[:, SLAB:2 * SLAB] + sums[:, 2 * SLAB:]
    carry[...] = jnp.broadcast_to(f[tm - 1:tm, :], carry.shape)
    f2 = f * LOG2E
    fT_ref[0] = f2.T

    lane = lax.broadcasted_iota(jnp.int32, (tm, SLAB), 1)
    extras = jnp.dot(_pieces(f2, lane), selk_ref[...], preferred_element_type=F32)

    low = lane < HEAD_DIM
    for j in range(N_HEADS // 2):
        ks = k[:, j * SLAB:(j + 1) * SLAB]
        k2 = ks * ks
        ms_lo = jnp.sum(jnp.where(low, k2, 0.0), axis=-1, keepdims=True) * (1.0 / HEAD_DIM)
        ms_hi = jnp.sum(jnp.where(low, 0.0, k2), axis=-1, keepdims=True) * (1.0 / HEAD_DIM)
        kn = ks * jnp.where(low, lax.rsqrt(ms_lo + EPS), lax.rsqrt(ms_hi + EPS)) * kg_ref[...]
        ex = extras[:, j * SLAB:(j + 1) * SLAB]
        k_ref[0, 2 * j] = jnp.where(low, kn, ex).astype(BF16)
        k_ref[0, 2 * j + 1] = jnp.where(low, ex, kn).astype(BF16)

    row = lax.broadcasted_iota(jnp.int32, (V_ROWS - HEAD_DIM, tm), 0)
    ones_blk = jnp.where(row == 0, 1.0, 0.0).astype(BF16)
    for hd in range(N_HEADS):
        vT_ref[0, hd, 0, 0:HEAD_DIM, :] = vT[hd * HEAD_DIM:(hd + 1) * HEAD_DIM, :].astype(BF16)
        vT_ref[0, hd, 0, HEAD_DIM:V_ROWS, :] = ones_blk


def _kv_call(x, g, w_k, w_vT, w_f3, b_f3, k_g2, sel_k, batch, seq):
    n, d = x.shape
    tm = TOKEN_TILE
    tps = seq // tm
    tpk = KV_TILE // tm
    tri = jnp.asarray(np.tril(np.ones((tm, tm), np.float32)), BF16)
    return pl.pallas_call(
        functools.partial(_kv_kernel, tps),
        grid=(n // tm,),
        in_specs=[
            pl.BlockSpec((tm, d), lambda t: (t, 0)),
            _resident((1, d), lambda t: (0, 0)),
            _resident((d, d), lambda t: (0, 0)),
            _resident((d, d), lambda t: (0, 0)),
            _resident((d, SLAB), lambda t: (0, 0)),
            _resident((1, SLAB), lambda t: (0, 0)),
            _resident((1, SLAB), lambda t: (0, 0)),
            _resident((SLAB, N_HEADS // 2 * SLAB), lambda t: (0, 0)),
            _resident((tm, tm), lambda t: (0, 0)),
        ],
        out_specs=[
            pl.BlockSpec((1, N_HEADS, tm, SLAB), lambda t: (t // tps, 0, t % tps, 0)),
            pl.BlockSpec((1, N_HEADS, 1, V_ROWS, tm),
                         lambda t: (t // tps, 0, (t % tps) // tpk, 0, t % tpk)),
            pl.BlockSpec((1, SLAB, tm), lambda t: (t // tps, 0, t % tps)),
        ],
        out_shape=[
            jax.ShapeDtypeStruct((batch, N_HEADS, seq, SLAB), BF16),
            jax.ShapeDtypeStruct((batch, N_HEADS, seq // KV_TILE, V_ROWS, KV_TILE), BF16),
            jax.ShapeDtypeStruct((batch, SLAB, seq), F32),
        ],
        scratch_shapes=[pltpu.VMEM((SUBLANES, SLAB), F32)],
        compiler_params=_cparams(1),
        name="shared_kv",
    )(x, g.reshape(1, d), w_k, w_vT, w_f3, b_f3, k_g2, sel_k, tri)


def _q_kernel(shift_ref, x_ref, mod_ref, g_ref, qwT_ref, qg_ref, fT_ref, selq_ref, o_ref):
    tm, d = x_ref.shape
    row = lax.broadcasted_iota(jnp.int32, (SLAB, tm), 0)
    extras = jnp.dot(selq_ref[...], _pieces(fT_ref[0] - shift_ref[0], row),
                     preferred_element_type=F32)
    h = _norm_mod(x_ref[...], g_ref[...], mod_ref[0], mod_ref[1]).astype(BF16)
    qT = lax.dot_general(qwT_ref[...], h, NT_DIMS, preferred_element_type=F32)
    for hd in range(N_HEADS):
        qh = qT[hd * HEAD_DIM:(hd + 1) * HEAD_DIM, :]
        ms = jnp.mean(qh * qh, axis=0, keepdims=True)
        qn = qh * lax.rsqrt(ms + EPS) * qg_ref[...]
        o_ref[0, hd, 0:HEAD_DIM, :] = qn.astype(BF16)
        o_ref[0, hd, HEAD_DIM:Q_ROWS, :] = (
            extras[hd * EXTRA_ROWS:(hd + 1) * EXTRA_ROWS, :].astype(BF16))


def _q_call(shift, x, mod_l, g, q_wT, layer, q_g_cols, fT, sel_q, batch, seq):
    n, d = x.shape
    tm = Q_TOKEN_TILE
    tps = seq // tm
    return pl.pallas_call(
        _q_kernel,
        grid=(n // tm,),
        in_specs=[
            pl.BlockSpec(memory_space=pltpu.SMEM),
            pl.BlockSpec((tm, d), lambda t: (t, 0)),
            pl.BlockSpec((None, N_MOD, 1, d), lambda t: (t // tps, 0, 0, 0)),
            _resident((1, d), lambda t: (0, 0)),
            _layer_spec(q_wT, layer),
            _resident((HEAD_DIM, tm), lambda t: (0, 0)),
            pl.BlockSpec((1, SLAB, tm), lambda t: (t // tps, 0, t % tps)),
            _resident((N_HEADS * EXTRA_ROWS, SLAB), lambda t: (0, 0)),
        ],
        out_specs=pl.BlockSpec((1, N_HEADS, Q_ROWS, tm), lambda t: (t // tps, 0, 0, t % tps)),
        out_shape=jax.ShapeDtypeStruct((batch, N_HEADS, Q_ROWS, seq), BF16),
        compiler_params=_cparams(1),
        name="fox_query",
    )(shift, x, mod_l, g.reshape(1, d), q_wT, q_g_cols, fT, sel_q)


def _store_normalized(acc_ref, o_ref):
    for hh in range(2):
        inv = 1.0 / acc_ref[hh, HEAD_DIM:HEAD_DIM + 1, :]
        o_ref[0, hh * HEAD_DIM:(hh + 1) * HEAD_DIM, :] = (
            acc_ref[hh, 0:HEAD_DIM, :] * inv).astype(BF16)


def _expand_queries(qc_ref, q_ref):
    tq = q_ref.shape[-1]
    zeros = jnp.zeros((HEAD_DIM - EXTRA_ROWS, tq), BF16)
    q_ref[0, 0:Q_ROWS, :] = qc_ref[0, 0]
    q_ref[0, Q_ROWS:SLAB, :] = zeros
    q_ref[1, 0:EXTRA_ROWS, :] = qc_ref[0, 1, HEAD_DIM:Q_ROWS, :]
    q_ref[1, EXTRA_ROWS:HEAD_DIM, :] = zeros
    q_ref[1, HEAD_DIM:SLAB, :] = qc_ref[0, 1, 0:HEAD_DIM, :]


def _attn_kernel(qc_ref, k_ref, vT_ref, o_ref, acc_ref, m_ref, qT_ref):
    tq = qT_ref.shape[-1]
    tk = vT_ref.shape[-1]
    i = pl.program_id(2)
    n_kv = (i * tq + tq + tk - 1) // tk

    _expand_queries(qc_ref, qT_ref)
    for hh in range(2):
        acc_ref[hh] = jnp.zeros((V_ROWS, tq), F32)
        m_ref[hh] = jnp.full((SUBLANES, tq), MASK_VALUE, F32)

    def step(kb, masked):
        start = pl.multiple_of(kb * tk, tk)
        for hh in range(2):
            s = jnp.dot(k_ref[0, hh, pl.ds(start, tk), :], qT_ref[hh],
                        preferred_element_type=F32)
            if masked:
                key_pos = kb * tk + lax.broadcasted_iota(jnp.int32, (tk, tq), 0)
                q_pos = i * tq + lax.broadcasted_iota(jnp.int32, (tk, tq), 1)
                s = jnp.where(key_pos <= q_pos, s, MASK_VALUE)
            m_old = m_ref[hh, 0:1, :]
            m_new = jnp.maximum(m_old, jnp.max(s, axis=0, keepdims=True))
            alpha = jnp.exp2(m_old - m_new)
            p = jnp.exp2(s - m_new).astype(BF16)
            pv = jnp.dot(vT_ref[0, hh, kb], p, preferred_element_type=F32)
            acc_ref[hh] = alpha * acc_ref[hh] + pv
            m_ref[hh] = jnp.broadcast_to(m_new, (SUBLANES, tq))

    def body(kb, carry):
        step(kb, False)
        return carry

    lax.fori_loop(0, n_kv - 1, body, 0)
    step(n_kv - 1, True)

    _store_normalized(acc_ref, o_ref)


def _attn_preshifted_kernel(qc_ref, k_ref, vT_ref, o_ref, acc_ref, p_ref, qT_ref):
    tq = qT_ref.shape[-1]
    tk = vT_ref.shape[-1]
    assert tq == tk
    i = pl.program_id(2)

    def probabilities(kb, slot):
        start = pl.multiple_of(kb * tk, tk)
        for hh in range(2):
            s = jnp.dot(k_ref[0, hh, pl.ds(start, tk), :], qT_ref[hh],
                        preferred_element_type=F32)
            p_ref[slot, hh] = jnp.exp2(s).astype(BF16)

    def diagonal_probabilities(slot):
        causal = (lax.broadcasted_iota(jnp.int32, (DIAG_BAND, tq), 0)
                  <= lax.broadcasted_iota(jnp.int32, (DIAG_BAND, tq), 1))
        for hh in range(2):
            q = qT_ref[hh]
            for r in range(tk // DIAG_BAND):
                lo = r * DIAG_BAND
                start = pl.multiple_of(i * tk + lo, DIAG_BAND)
                s = jnp.dot(k_ref[0, hh, pl.ds(start, DIAG_BAND), :], q[:, lo:],
                            preferred_element_type=F32)
                p_ref[slot, hh, lo:lo + DIAG_BAND, lo:] = jnp.exp2(
                    jnp.where(causal[:, :tq - lo], s, MASK_VALUE)).astype(BF16)

    def accumulate(kb, slot):
        for hh in range(2):
            acc_ref[hh] += jnp.dot(vT_ref[0, hh, kb], p_ref[slot, hh],
                                   preferred_element_type=F32)

    def accumulate_diagonal(slot):
        for hh in range(2):
            for c in range(tq // DIAG_BAND):
                hi = (c + 1) * DIAG_BAND
                cols = slice(c * DIAG_BAND, hi)
                acc_ref[hh, :, cols] += jnp.dot(vT_ref[0, hh, i, :, 0:hi], p_ref[slot, hh, 0:hi, cols],
                                                preferred_element_type=F32)

    _expand_queries(qc_ref, qT_ref)
    for hh in range(2):
        acc_ref[hh] = jnp.zeros((V_ROWS, tq), F32)

    rest = jnp.maximum(i - 1, 0)
    odd = rest % 2

    @pl.when(i >= 1)
    def _():
        probabilities(0, 0)

    @pl.when(odd == 1)
    def _():
        accumulate(0, 0)
        probabilities(1, 0)

    def body(t, pending):
        kb = 1 + odd + 2 * t
        accumulate(pending, 0)
        probabilities(kb, 1)
        probabilities(kb + 1, 0)
        accumulate(kb, 1)
        return kb + 1

    pending = lax.fori_loop(0, rest // 2, body, odd)

    @pl.when(i >= 1)
    def _():
        diagonal_probabilities(1)
        accumulate(pending, 0)
        accumulate_diagonal(1)

    @pl.when(i == 0)
    def _():
        diagonal_probabilities(1)
        accumulate_diagonal(1)

    _store_normalized(acc_ref, o_ref)


def _attn_call(qT, k, vT, batch, seq, preshifted):
    nkb = seq // KV_TILE
    pairs = N_HEADS // 2
    if preshifted:
        body = _attn_preshifted_kernel
        scratch = [pltpu.VMEM((2, V_ROWS, Q_TILE), F32), pltpu.VMEM((2, 2, KV_TILE, Q_TILE), BF16)]
    else:
        body = _attn_kernel
        scratch = [pltpu.VMEM((2, V_ROWS, Q_TILE), F32), pltpu.VMEM((2, SUBLANES, Q_TILE), F32)]
    scratch.append(pltpu.VMEM((2, SLAB, Q_TILE), BF16))
    return pl.pallas_call(
        body,
        grid=(batch, pairs, seq // Q_TILE),
        in_specs=[
            pl.BlockSpec((1, 2, Q_ROWS, Q_TILE), lambda b, j, i: (b, j, 0, i)),
            pl.BlockSpec((1, 2, seq, SLAB), lambda b, j, i: (b, j, 0, 0)),
            pl.BlockSpec((1, 2, nkb, V_ROWS, KV_TILE), lambda b, j, i: (b, j, 0, 0, 0)),
        ],
        out_specs=pl.BlockSpec((1, SLAB, Q_TILE), lambda b, j, i: (b, j, i)),
        out_shape=jax.ShapeDtypeStruct((batch, N_HEADS * HEAD_DIM, seq), BF16),
        scratch_shapes=scratch,
        compiler_params=_cparams(3),
        name="fox_attention_preshifted" if preshifted else "fox_attention",
    )(qT, k, vT)


def kernel(x, c, ada_w, ada_b, norm_mix_g, norm_mlp_g, sc_w_in, sc_conv, sc_w_out,
           kv_norm_g, w_kv, k_norm_g, w_f, b_f, q_w, q_norm_g, o_w, mlp_w1, mlp_w2):
    batch, seq, d = x.shape
    depth = ada_w.shape[0]
    n_conv = sc_w_in.shape[0]
    assert d == N_HEADS * HEAD_DIM and seq % KV_TILE == 0 and seq % Q_TOKEN_TILE == 0
    assert KV_TILE % TOKEN_TILE == 0 and KV_TILE == Q_TILE

    mod = _ada_call(c, ada_w, ada_b)
    sel_k, sel_q = _selection_matrices()
    q_scale = LOG2E / np.sqrt(HEAD_DIM)

    q_wT_all, o_w_all = jnp.swapaxes(q_w, 1, 2).astype(BF16), o_w.astype(BF16)
    conv_w_bf16 = {0: [sc_w_in[0].astype(BF16), sc_w_out[0].astype(BF16)]} if n_conv else {}
    mlp_w_bf16 = {} if n_conv else {0: [mlp_w1[0].astype(BF16), mlp_w2[0].astype(BF16)]}

    def jobs_for_next(l):
        if l + 1 >= depth:
            return []
        return [(sc_w_in, l + 1), (sc_w_out, l + 1)] if l + 1 < n_conv \
            else [(mlp_w1, l + 1), (mlp_w2, l + 1)]

    def keep_casts(l, casts):
        if casts:
            (conv_w_bf16 if l + 1 < n_conv else mlp_w_bf16)[l + 1] = casts

    xs = x.reshape(batch * seq, d)
    k = vT = fT = None
    for l in range(depth):
        if l < n_conv:
            xs, mlp_w_bf16[l] = _conv_call(xs, mod[l], norm_mix_g[l], conv_w_bf16[l][0], sc_conv[l],
                                           conv_w_bf16[l][1], seq,
                                           cast_jobs=[(mlp_w1, l), (mlp_w2, l)])
            xs, casts = _mlp_call(xs, mod[l], norm_mlp_g[l], *mlp_w_bf16[l], seq,
                                  cast_jobs=jobs_for_next(l))
            keep_casts(l, casts)
        else:
            i = l - n_conv
            q_g_cols = jnp.broadcast_to((q_norm_g[i] * q_scale)[:, None], (HEAD_DIM, Q_TOKEN_TILE))
            bound = (np.sqrt(HEAD_DIM) * BOUND_SLACK) * jnp.max(jnp.abs(q_norm_g[i])) \
                * jnp.max(jnp.abs(k_norm_g))
            preshift_ok = bound <= MAX_PRESHIFT
            shift = jnp.where(preshift_ok, bound * LOG2E, 0.0).reshape(1).astype(F32)
            qT = _q_call(shift, xs, mod[l], norm_mix_g[l], q_wT_all, i, q_g_cols, fT,
                         sel_q, batch, seq)
            attn = lax.cond(
                preshift_ok,
                functools.partial(_attn_call, batch=batch, seq=seq, preshifted=True),
                functools.partial(_attn_call, batch=batch, seq=seq, preshifted=False),
                qT, k, vT)
            xs, casts = _mlp_call(xs, mod[l], norm_mlp_g[l], *mlp_w_bf16[l], seq,
                                  attn=attn, o_w=o_w_all, o_layer=i, cast_jobs=jobs_for_next(l))
            keep_casts(l, casts)
        if l == n_conv - 1:
            w_f3 = jnp.zeros((d, SLAB), F32).at[:, :3 * N_HEADS].set(jnp.tile(w_f, (1, 3)))
            b_f3 = jnp.zeros((1, SLAB), F32).at[0, :3 * N_HEADS].set(jnp.tile(b_f, 3))
            k, vT, fT = _kv_call(xs, kv_norm_g, w_kv[:, :d].astype(BF16),
                                 w_kv[:, d:].T.astype(BF16), w_f3.astype(BF16), b_f3,
                                 jnp.tile(k_norm_g, 2).reshape(1, SLAB), sel_k, batch, seq)
    return xs.reshape(batch, seq, d)
```

```python
import functools

import numpy as np
import jax
import jax.numpy as jnp
from jax import lax
from jax.experimental import pallas as pl
from jax.experimental.pallas import tpu as pltpu

F32 = jnp.float32
BF16 = jnp.bfloat16

SUBLANES = 8
N_HEADS = 16
HEAD_DIM = 64
SLAB = 2 * HEAD_DIM
V_ROWS = 2 * HEAD_DIM
N_MOD = 6
CONV_W = 3
EPS = 1e-6
MASK_VALUE = -1e30

TOKEN_TILE = 512
Q_TOKEN_TILE = 1024
FF_CHUNK = 1024
Q_TILE = 1024
KV_TILE = 1024
DIAG_BAND = 256
ADA_COLS = 1536
LOG2E = 1.4426950408889634
MAX_PRESHIFT = 40.0
BOUND_SLACK = 1.0 + 2.0 ** -6
V7X_VMEM_BYTES = 64 * 1024 * 1024
VMEM_LIMIT_BYTES = V7X_VMEM_BYTES - 8 * 1024 * 1024

NT_DIMS = (((1,), (1,)), ((), ()))
TN_DIMS = (((0,), (0,)), ((), ()))


def _cparams(n_axes):
    return pltpu.CompilerParams(
        dimension_semantics=("arbitrary",) * n_axes,
        vmem_limit_bytes=VMEM_LIMIT_BYTES)


def _resident(block_shape, index_map):
    return pl.BlockSpec(block_shape, index_map, pipeline_mode=pl.Buffered(1))


def _layer_spec(w, layer=None):
    if w.ndim == 2:
        return _resident(w.shape, lambda *_: (0, 0))
    _, rows, cols = w.shape
    return _resident((None, rows, cols), lambda *_: (layer, 0, 0))


def _with_cast_jobs(body, n_in, n_out, n_jobs):
    def kernel(*refs):
        ins, rest = refs[:n_in], refs[n_in:]
        job_src, rest = rest[:n_jobs], rest[n_jobs:]
        outs, rest = rest[:n_out], rest[n_out:]
        job_dst, scratch = rest[:n_jobs], rest[n_jobs:]
        body(*ins, *outs, *scratch)
        for src, dst in zip(job_src, job_dst):
            dst[...] = src[...].astype(BF16)
    return kernel


def _cast_job_specs(jobs, n_steps):
    in_specs, out_specs, out_shapes, operands = [], [], [], []
    for stack, layer in jobs:
        _, rows, cols = stack.shape
        slab = rows // n_steps
        assert slab * n_steps == rows and slab % 16 == 0
        in_specs.append(pl.BlockSpec((None, slab, cols), lambda t, layer=layer: (layer, t, 0)))
        out_specs.append(pl.BlockSpec((slab, cols), lambda t: (t, 0)))
        out_shapes.append(jax.ShapeDtypeStruct((rows, cols), BF16))
        operands.append(stack)
    return in_specs, out_specs, out_shapes, operands


def _rms_scale(x):
    ms = jnp.mean(x * x, axis=-1, keepdims=True)
    return x * lax.rsqrt(ms + EPS)


def _norm_mod(x, g, shift, scale):
    return (_rms_scale(x) * g) * (1.0 + scale) + shift


def _split3(v):
    hi = v.astype(BF16).astype(F32)
    r = v - hi
    mid = r.astype(BF16).astype(F32)
    lo = (r - mid).astype(BF16).astype(F32)
    return hi, mid, lo


def _ada_kernel(c_ref, w_ref, b_ref, o_ref):
    c = c_ref[...]
    ca = c * jax.nn.sigmoid(c)
    o_ref[0] = jnp.dot(ca.astype(BF16), w_ref[0].astype(BF16),
                       preferred_element_type=F32) + b_ref[0]


def _ada_call(c, ada_w, ada_b):
    depth, d, nm = ada_w.shape
    b = c.shape[0]
    rows = SUBLANES
    c_pad = jnp.zeros((rows, d), F32).at[:b].set(c)
    out = pl.pallas_call(
        _ada_kernel,
        grid=(depth, nm // ADA_COLS),
        in_specs=[
            pl.BlockSpec((rows, d), lambda l, n: (0, 0)),
            pl.BlockSpec((1, d, ADA_COLS), lambda l, n: (l, 0, n)),
            pl.BlockSpec((1, 1, ADA_COLS), lambda l, n: (l, 0, n)),
        ],
        out_specs=pl.BlockSpec((1, rows, ADA_COLS), lambda l, n: (l, 0, n)),
        out_shape=jax.ShapeDtypeStruct((depth, rows, nm), F32),
        compiler_params=_cparams(2),
        name="ada_mod",
    )(c_pad, ada_w, ada_b.reshape(depth, 1, nm))
    return out[:, :b, :].reshape(depth, b, N_MOD, 1, d)


def _conv_kernel(tiles_per_seq, x_ref, mod_ref, g_ref, win_ref, cw_ref, wout_ref,
                 o_ref, zbuf):
    tm, d = x_ref.shape
    t = pl.program_id(0)

    @pl.when(t % tiles_per_seq == 0)
    def _():
        zbuf[0:SUBLANES, :] = jnp.zeros((SUBLANES, d), F32)

    x = x_ref[...]
    h = _norm_mod(x, g_ref[...], mod_ref[0], mod_ref[1]).astype(BF16)
    cx = jnp.dot(h, win_ref[:, d:], preferred_element_type=F32)
    z = cx[:, :d] * cx[:, d:]
    zbuf[SUBLANES:SUBLANES + tm, :] = z
    gate = jnp.dot(h, win_ref[:, :d], preferred_element_type=F32)
    z1 = zbuf[SUBLANES - 1:SUBLANES - 1 + tm, :]
    z2 = zbuf[SUBLANES - 2:SUBLANES - 2 + tm, :]
    zc = cw_ref[0] * z2 + cw_ref[1] * z1 + cw_ref[2] * z
    zbuf[0:SUBLANES, :] = z[tm - SUBLANES:, :]
    gated = (gate * zc).astype(BF16)
    y = jnp.dot(gated, wout_ref[...], preferred_element_type=F32)
    o_ref[...] = x + mod_ref[2] * y


def _conv_call(x, mod_l, g, w_in, conv_w, w_out, seq, cast_jobs=()):
    n, d = x.shape
    tm = TOKEN_TILE
    tps = seq // tm
    steps = n // tm
    job_in, job_out, job_shapes, job_args = _cast_job_specs(cast_jobs, steps)
    in_specs = [
        pl.BlockSpec((tm, d), lambda t: (t, 0)),
        pl.BlockSpec((None, N_MOD, 1, d), lambda t: (t // tps, 0, 0, 0)),
        _resident((1, d), lambda t: (0, 0)),
        _layer_spec(w_in),
        _resident((CONV_W, 1, d), lambda t: (0, 0, 0)),
        _layer_spec(w_out),
    ]
    outs = pl.pallas_call(
        _with_cast_jobs(functools.partial(_conv_kernel, tps), len(in_specs), 1, len(cast_jobs)),
        grid=(steps,),
        in_specs=in_specs + job_in,
        out_specs=[pl.BlockSpec((tm, d), lambda t: (t, 0))] + job_out,
        out_shape=[jax.ShapeDtypeStruct((n, d), F32)] + job_shapes,
        scratch_shapes=[pltpu.VMEM((tm + SUBLANES, d), F32)],
        compiler_params=_cparams(1),
        name="conv_mixer",
    )(x, mod_l, g.reshape(1, d), w_in, conv_w.reshape(CONV_W, 1, d), w_out, *job_args)
    return outs[0], list(outs[1:])


def _mlp_residual(x1, mod_ref, g, w1_ref, w2_ref):
    d, ff = w1_ref.shape
    h = _norm_mod(x1, g, mod_ref[3], mod_ref[4]).astype(BF16)
    acc = jnp.zeros(x1.shape, F32)
    for c in range(ff // FF_CHUNK):
        cols = slice(c * FF_CHUNK, (c + 1) * FF_CHUNK)
        a = jnp.maximum(jnp.dot(h, w1_ref[:, cols], preferred_element_type=F32), 0.0)
        acc = acc + jnp.dot((a * a).astype(BF16), w2_ref[cols, :],
                            preferred_element_type=F32)
    return x1 + mod_ref[5] * acc


def _mlp_kernel(x_ref, mod_ref, g_ref, w1_ref, w2_ref, o_ref):
    o_ref[...] = _mlp_residual(x_ref[...], mod_ref, g_ref[...], w1_ref, w2_ref)


def _omlp_kernel(x_ref, aT_ref, mod_ref, g_ref, ow_ref, w1_ref, w2_ref, o_ref):
    y = lax.dot_general(aT_ref[0], ow_ref[...], TN_DIMS, preferred_element_type=F32)
    x1 = x_ref[...] + mod_ref[2] * y
    o_ref[...] = _mlp_residual(x1, mod_ref, g_ref[...], w1_ref, w2_ref)


def _mlp_call(x, mod_l, g, w1, w2, seq, attn=None, o_w=None, o_layer=None, cast_jobs=()):
    n, d = x.shape
    tm = TOKEN_TILE
    tps = seq // tm
    steps = n // tm
    job_in, job_out, job_shapes, job_args = _cast_job_specs(cast_jobs, steps)
    tile = pl.BlockSpec((tm, d), lambda t: (t, 0))
    mod_spec = pl.BlockSpec((None, N_MOD, 1, d), lambda t: (t // tps, 0, 0, 0))
    g_spec = _resident((1, d), lambda t: (0, 0))
    w_specs = [_layer_spec(w1), _layer_spec(w2)]
    if attn is None:
        body, name = _mlp_kernel, "mlp"
        in_specs = [tile, mod_spec, g_spec] + w_specs
        args = (x, mod_l, g.reshape(1, d), w1, w2)
    else:
        body, name = _omlp_kernel, "oproj_mlp"
        attn_spec = pl.BlockSpec((1, d, tm), lambda t: (t // tps, 0, t % tps))
        in_specs = [tile, attn_spec, mod_spec, g_spec, _layer_spec(o_w, o_layer)] + w_specs
        args = (x, attn, mod_l, g.reshape(1, d), o_w, w1, w2)
    outs = pl.pallas_call(
        _with_cast_jobs(body, len(in_specs), 1, len(cast_jobs)),
        grid=(steps,),
        in_specs=in_specs + job_in,
        out_specs=[tile] + job_out,
        out_shape=[jax.ShapeDtypeStruct((n, d), F32)] + job_shapes,
        compiler_params=_cparams(1),
        name=name,
    )(*args, *job_args)
    return outs[0], list(outs[1:])


def _extra_base(head):
    return HEAD_DIM if head % 2 == 0 else 0


ONE_LANE = SLAB - 1


EXTRA_ROWS = 16
Q_ROWS = HEAD_DIM + EXTRA_ROWS


def _selection_matrices():
    sel_k = np.zeros((SLAB, N_HEADS // 2 * SLAB), np.float32)
    sel_q = np.zeros((N_HEADS * EXTRA_ROWS, SLAB), np.float32)
    for h in range(N_HEADS):
        base = (h // 2) * SLAB + _extra_base(h)
        for piece in range(3):
            sel_q[h * EXTRA_ROWS + piece, piece * N_HEADS + h] = 1.0
            sel_q[h * EXTRA_ROWS + 3 + piece, ONE_LANE] = 1.0
            sel_k[ONE_LANE, base + piece] = 1.0
            sel_k[piece * N_HEADS + h, base + 3 + piece] = -1.0
    return jnp.asarray(sel_k, BF16), jnp.asarray(sel_q, BF16)


def _pieces(f, index):
    hi, mid, lo = _split3(f)
    one = jnp.where(index == ONE_LANE, 1.0, 0.0)
    return jnp.where(index < N_HEADS, hi,
                     jnp.where(index < 2 * N_HEADS, mid,
                               jnp.where(index < 3 * N_HEADS, lo, one))).astype(BF16)


def _kv_kernel(tiles_per_seq, x_ref, g_ref, wk_ref, wvT_ref, wf_ref, bf_ref, kg_ref,
               selk_ref, tri_ref, k_ref, vT_ref, fT_ref, carry):
    tm, d = x_ref.shape
    t = pl.program_id(0)

    @pl.when(t % tiles_per_seq == 0)
    def _():
        carry[...] = jnp.zeros(carry.shape, F32)

    h = (_rms_scale(x_ref[...]) * g_ref[...]).astype(BF16)

    xf = jnp.dot(h, wf_ref[...], preferred_element_type=F32) + bf_ref[...]
    k = jnp.dot(h, wk_ref[...], preferred_element_type=F32)
    log_f = jnp.minimum(xf, 0.0) - jnp.log1p(jnp.exp(-jnp.abs(xf)))
    pieces = jnp.concatenate([p.astype(BF16) for p in _split3(log_f)], axis=-1)
    sums = jnp.dot(tri_ref[...], pieces, preferred_element_type=F32)
    vT = lax.dot_general(wvT_ref[...], h, NT_DIMS, preferred_element_type=F32)
    f = carry[0:1, :] + sums[:, 0:SLAB] + sums[:, SLAB:2 * SLAB] + sums[:, 2 * SLAB:]
    carry[...] = jnp.broadcast_to(f[tm - 1:tm, :], carry.shape)
    f2 = f * LOG2E
    fT_ref[0] = f2.T

    lane = lax.broadcasted_iota(jnp.int32, (tm, SLAB), 1)
    extras = jnp.dot(_pieces(f2, lane), selk_ref[...], preferred_element_type=F32)

    low = lane < HEAD_DIM
    for j in range(N_HEADS // 2):
        ks = k[:, j * SLAB:(j + 1) * SLAB]
        k2 = ks * ks
        ms_lo = jnp.sum(jnp.where(low, k2, 0.0), axis=-1, keepdims=True) * (1.0 / HEAD_DIM)
        ms_hi = jnp.sum(jnp.where(low, 0.0, k2), axis=-1, keepdims=True) * (1.0 / HEAD_DIM)
        kn = ks * jnp.where(low, lax.rsqrt(ms_lo + EPS), lax.rsqrt(ms_hi + EPS)) * kg_ref[...]
        ex = extras[:, j * SLAB:(j + 1) * SLAB]
        k_ref[0, 2 * j] = jnp.where(low, kn, ex).astype(BF16)
        k_ref[0, 2 * j + 1] = jnp.where(low, ex, kn).astype(BF16)

    row = lax.broadcasted_iota(jnp.int32, (V_ROWS - HEAD_DIM, tm), 0)
    ones_blk = jnp.where(row == 0, 1.0, 0.0).astype(BF16)
    for hd in range(N_HEADS):
        vT_ref[0, hd, 0, 0:HEAD_DIM, :] = vT[hd * HEAD_DIM:(hd + 1) * HEAD_DIM, :].astype(BF16)
        vT_ref[0, hd, 0, HEAD_DIM:V_ROWS, :] = ones_blk


def _kv_call(x, g, w_k, w_vT, w_f3, b_f3, k_g2, sel_k, batch, seq):
    n, d = x.shape
    tm = TOKEN_TILE
    tps = seq // tm
    tpk = KV_TILE // tm
    tri = jnp.asarray(np.tril(np.ones((tm, tm), np.float32)), BF16)
    return pl.pallas_call(
        functools.partial(_kv_kernel, tps),
        grid=(n // tm,),
        in_specs=[
            pl.BlockSpec((tm, d), lambda t: (t, 0)),
            _resident((1, d), lambda t: (0, 0)),
            _resident((d, d), lambda t: (0, 0)),
            _resident((d, d), lambda t: (0, 0)),
            _resident((d, SLAB), lambda t: (0, 0)),
            _resident((1, SLAB), lambda t: (0, 0)),
            _resident((1, SLAB), lambda t: (0, 0)),
            _resident((SLAB, N_HEADS // 2 * SLAB), lambda t: (0, 0)),
            _resident((tm, tm), lambda t: (0, 0)),
        ],
        out_specs=[
            pl.BlockSpec((1, N_HEADS, tm, SLAB), lambda t: (t // tps, 0, t % tps, 0)),
            pl.BlockSpec((1, N_HEADS, 1, V_ROWS, tm),
                         lambda t: (t // tps, 0, (t % tps) // tpk, 0, t % tpk)),
            pl.BlockSpec((1, SLAB, tm), lambda t: (t // tps, 0, t % tps)),
        ],
        out_shape=[
            jax.ShapeDtypeStruct((batch, N_HEADS, seq, SLAB), BF16),
            jax.ShapeDtypeStruct((batch, N_HEADS, seq // KV_TILE, V_ROWS, KV_TILE), BF16),
            jax.ShapeDtypeStruct((batch, SLAB, seq), F32),
        ],
        scratch_shapes=[pltpu.VMEM((SUBLANES, SLAB), F32)],
        compiler_params=_cparams(1),
        name="shared_kv",
    )(x, g.reshape(1, d), w_k, w_vT, w_f3, b_f3, k_g2, sel_k, tri)


def _q_kernel(shift_ref, x_ref, mod_ref, g_ref, qwT_ref, qg_ref, fT_ref, selq_ref, o_ref):
    tm, d = x_ref.shape
    row = lax.broadcasted_iota(jnp.int32, (SLAB, tm), 0)
    extras = jnp.dot(selq_ref[...], _pieces(fT_ref[0] - shift_ref[0], row),
                     preferred_element_type=F32)
    h = _norm_mod(x_ref[...], g_ref[...], mod_ref[0], mod_ref[1]).astype(BF16)
    qT = lax.dot_general(qwT_ref[...], h, NT_DIMS, preferred_element_type=F32)
    for hd in range(N_HEADS):
        qh = qT[hd * HEAD_DIM:(hd + 1) * HEAD_DIM, :]
        ms = jnp.mean(qh * qh, axis=0, keepdims=True)
        qn = qh * lax.rsqrt(ms + EPS) * qg_ref[...]
        o_ref[0, hd, 0:HEAD_DIM, :] = qn.astype(BF16)
        o_ref[0, hd, HEAD_DIM:Q_ROWS, :] = (
            extras[hd * EXTRA_ROWS:(hd + 1) * EXTRA_ROWS, :].astype(BF16))


def _q_call(shift, x, mod_l, g, q_wT, layer, q_g_cols, fT, sel_q, batch, seq):
    n, d = x.shape
    tm = Q_TOKEN_TILE
    tps = seq // tm
    return pl.pallas_call(
        _q_kernel,
        grid=(n // tm,),
        in_specs=[
            pl.BlockSpec(memory_space=pltpu.SMEM),
            pl.BlockSpec((tm, d), lambda t: (t, 0)),
            pl.BlockSpec((None, N_MOD, 1, d), lambda t: (t // tps, 0, 0, 0)),
            _resident((1, d), lambda t: (0, 0)),
            _layer_spec(q_wT, layer),
            _resident((HEAD_DIM, tm), lambda t: (0, 0)),
            pl.BlockSpec((1, SLAB, tm), lambda t: (t // tps, 0, t % tps)),
            _resident((N_HEADS * EXTRA_ROWS, SLAB), lambda t: (0, 0)),
        ],
        out_specs=pl.BlockSpec((1, N_HEADS, Q_ROWS, tm), lambda t: (t // tps, 0, 0, t % tps)),
        out_shape=jax.ShapeDtypeStruct((batch, N_HEADS, Q_ROWS, seq), BF16),
        compiler_params=_cparams(1),
        name="fox_query",
    )(shift, x, mod_l, g.reshape(1, d), q_wT, q_g_cols, fT, sel_q)


def _store_normalized(acc_ref, o_ref):
    for hh in range(2):
        inv = 1.0 / acc_ref[hh, HEAD_DIM:HEAD_DIM + 1, :]
        o_ref[0, hh * HEAD_DIM:(hh + 1) * HEAD_DIM, :] = (
            acc_ref[hh, 0:HEAD_DIM, :] * inv).astype(BF16)


def _expand_queries(qc_ref, q_ref):
    tq = q_ref.shape[-1]
    zeros = jnp.zeros((HEAD_DIM - EXTRA_ROWS, tq), BF16)
    q_ref[0, 0:Q_ROWS, :] = qc_ref[0, 0]
    q_ref[0, Q_ROWS:SLAB, :] = zeros
    q_ref[1, 0:EXTRA_ROWS, :] = qc_ref[0, 1, HEAD_DIM:Q_ROWS, :]
    q_ref[1, EXTRA_ROWS:HEAD_DIM, :] = zeros
    q_ref[1, HEAD_DIM:SLAB, :] = qc_ref[0, 1, 0:HEAD_DIM, :]


def _attn_kernel(qc_ref, k_ref, vT_ref, o_ref, acc_ref, m_ref, qT_ref):
    tq = qT_ref.shape[-1]
    tk = vT_ref.shape[-1]
    i = pl.program_id(2)
    n_kv = (i * tq + tq + tk - 1) // tk

    _expand_queries(qc_ref, qT_ref)
    for hh in range(2):
        acc_ref[hh] = jnp.zeros((V_ROWS, tq), F32)
        m_ref[hh] = jnp.full((SUBLANES, tq), MASK_VALUE, F32)

    def step(kb, masked):
        start = pl.multiple_of(kb * tk, tk)
        for hh in range(2):
            s = jnp.dot(k_ref[0, hh, pl.ds(start, tk), :], qT_ref[hh],
                        preferred_element_type=F32)
            if masked:
                key_pos = kb * tk + lax.broadcasted_iota(jnp.int32, (tk, tq), 0)
                q_pos = i * tq + lax.broadcasted_iota(jnp.int32, (tk, tq), 1)
                s = jnp.where(key_pos <= q_pos, s, MASK_VALUE)
            m_old = m_ref[hh, 0:1, :]
            m_new = jnp.maximum(m_old, jnp.max(s, axis=0, keepdims=True))
            alpha = jnp.exp2(m_old - m_new)
            p = jnp.exp2(s - m_new).astype(BF16)
            pv = jnp.dot(vT_ref[0, hh, kb], p, preferred_element_type=F32)
            acc_ref[hh] = alpha * acc_ref[hh] + pv
            m_ref[hh] = jnp.broadcast_to(m_new, (SUBLANES, tq))

    def body(kb, carry):
        step(kb, False)
        return carry

    lax.fori_loop(0, n_kv - 1, body, 0)
    step(n_kv - 1, True)

    _store_normalized(acc_ref, o_ref)


def _attn_preshifted_kernel(qc_ref, k_ref, vT_ref, o_ref, acc_ref, p_ref, qT_ref):
    tq = qT_ref.shape[-1]
    tk = vT_ref.shape[-1]
    assert tq == tk
    i = pl.program_id(2)

    def probabilities(kb, slot):
        start = pl.multiple_of(kb * tk, tk)
        for hh in range(2):
            s = jnp.dot(k_ref[0, hh, pl.ds(start, tk), :], qT_ref[hh],
                        preferred_element_type=F32)
            p_ref[slot, hh] = jnp.exp2(s).astype(BF16)

    def diagonal_probabilities(slot):
        causal = (lax.broadcasted_iota(jnp.int32, (DIAG_BAND, tq), 0)
                  <= lax.broadcasted_iota(jnp.int32, (DIAG_BAND, tq), 1))
        for hh in range(2):
            q = qT_ref[hh]
            for r in range(tk // DIAG_BAND):
                lo = r * DIAG_BAND
                start = pl.multiple_of(i * tk + lo, DIAG_BAND)
                s = jnp.dot(k_ref[0, hh, pl.ds(start, DIAG_BAND), :], q[:, lo:],
                            preferred_element_type=F32)
                p_ref[slot, hh, lo:lo + DIAG_BAND, lo:] = jnp.exp2(
                    jnp.where(causal[:, :tq - lo], s, MASK_VALUE)).astype(BF16)

    def accumulate(kb, slot):
        for hh in range(2):
            acc_ref[hh] += jnp.dot(vT_ref[0, hh, kb], p_ref[slot, hh],
                                   preferred_element_type=F32)

    def accumulate_diagonal(slot):
        for hh in range(2):
            for c in range(tq // DIAG_BAND):
                hi = (c + 1) * DIAG_BAND
                cols = slice(c * DIAG_BAND, hi)
                acc_ref[hh, :, cols] += jnp.dot(vT_ref[0, hh, i, :, 0:hi], p_ref[slot, hh, 0:hi, cols],
                                                preferred_element_type=F32)

    _expand_queries(qc_ref, qT_ref)
    for hh in range(2):
        acc_ref[hh] = jnp.zeros((V_ROWS, tq), F32)

    rest = jnp.maximum(i - 1, 0)
    odd = rest % 2

    @pl.when(i >= 1)
    def _():
        probabilities(0, 0)

    @pl.when(odd == 1)
    def _():
        accumulate(0, 0)
        probabilities(1, 0)

    def body(t, pending):
        kb = 1 + odd + 2 * t
        accumulate(pending, 0)
        probabilities(kb, 1)
        probabilities(kb + 1, 0)
        accumulate(kb, 1)
        return kb + 1

    pending = lax.fori_loop(0, rest // 2, body, odd)

    @pl.when(i >= 1)
    def _():
        diagonal_probabilities(1)
        accumulate(pending, 0)
        accumulate_diagonal(1)

    @pl.when(i == 0)
    def _():
        diagonal_probabilities(1)
        accumulate_diagonal(1)

    _store_normalized(acc_ref, o_ref)


def _attn_call(qT, k, vT, batch, seq, preshifted):
    nkb = seq // KV_TILE
    pairs = N_HEADS // 2
    if preshifted:
        body = _attn_preshifted_kernel
        scratch = [pltpu.VMEM((2, V_ROWS, Q_TILE), F32), pltpu.VMEM((2, 2, KV_TILE, Q_TILE), BF16)]
    else:
        body = _attn_kernel
        scratch = [pltpu.VMEM((2, V_ROWS, Q_TILE), F32), pltpu.VMEM((2, SUBLANES, Q_TILE), F32)]
    scratch.append(pltpu.VMEM((2, SLAB, Q_TILE), BF16))
    return pl.pallas_call(
        body,
        grid=(batch, pairs, seq // Q_TILE),
        in_specs=[
            pl.BlockSpec((1, 2, Q_ROWS, Q_TILE), lambda b, j, i: (b, j, 0, i)),
            pl.BlockSpec((1, 2, seq, SLAB), lambda b, j, i: (b, j, 0, 0)),
            pl.BlockSpec((1, 2, nkb, V_ROWS, KV_TILE), lambda b, j, i: (b, j, 0, 0, 0)),
        ],
        out_specs=pl.BlockSpec((1, SLAB, Q_TILE), lambda b, j, i: (b, j, i)),
        out_shape=jax.ShapeDtypeStruct((batch, N_HEADS * HEAD_DIM, seq), BF16),
        scratch_shapes=scratch,
        compiler_params=_cparams(3),
        name="fox_attention_preshifted" if preshifted else "fox_attention",
    )(qT, k, vT)


def kernel(x, c, ada_w, ada_b, norm_mix_g, norm_mlp_g, sc_w_in, sc_conv, sc_w_out,
           kv_norm_g, w_kv, k_norm_g, w_f, b_f, q_w, q_norm_g, o_w, mlp_w1, mlp_w2):
    batch, seq, d = x.shape
    depth = ada_w.shape[0]
    n_conv = sc_w_in.shape[0]
    assert d == N_HEADS * HEAD_DIM and seq % KV_TILE == 0 and seq % Q_TOKEN_TILE == 0
    assert KV_TILE % TOKEN_TILE == 0 and KV_TILE == Q_TILE

    mod = _ada_call(c, ada_w, ada_b)
    sel_k, sel_q = _selection_matrices()
    q_scale = LOG2E / np.sqrt(HEAD_DIM)

    q_wT_all, o_w_all = jnp.swapaxes(q_w, 1, 2).astype(BF16), o_w.astype(BF16)
    conv_w_bf16 = {0: [sc_w_in[0].astype(BF16), sc_w_out[0].astype(BF16)]} if n_conv else {}
    mlp_w_bf16 = {} if n_conv else {0: [mlp_w1[0].astype(BF16), mlp_w2[0].astype(BF16)]}

    def jobs_for_next(l):
        if l + 1 >= depth:
            return []
        return [(sc_w_in, l + 1), (sc_w_out, l + 1)] if l + 1 < n_conv \
            else [(mlp_w1, l + 1), (mlp_w2, l + 1)]

    def keep_casts(l, casts):
        if casts:
            (conv_w_bf16 if l + 1 < n_conv else mlp_w_bf16)[l + 1] = casts

    xs = x.reshape(batch * seq, d)
    k = vT = fT = None
    for l in range(depth):
        if l < n_conv:
            xs, mlp_w_bf16[l] = _conv_call(xs, mod[l], norm_mix_g[l], conv_w_bf16[l][0], sc_conv[l],
                                           conv_w_bf16[l][1], seq,
                                           cast_jobs=[(mlp_w1, l), (mlp_w2, l)])
            xs, casts = _mlp_call(xs, mod[l], norm_mlp_g[l], *mlp_w_bf16[l], seq,
                                  cast_jobs=jobs_for_next(l))
            keep_casts(l, casts)
        else:
            i = l - n_conv
            q_g_cols = jnp.broadcast_to((q_norm_g[i] * q_scale)[:, None], (HEAD_DIM, Q_TOKEN_TILE))
            bound = (np.sqrt(HEAD_DIM) * BOUND_SLACK) * jnp.max(jnp.abs(q_norm_g[i])) \
                * jnp.max(jnp.abs(k_norm_g))
            preshift_ok = bound <= MAX_PRESHIFT
            shift = jnp.where(preshift_ok, bound * LOG2E, 0.0).reshape(1).astype(F32)
            qT = _q_call(shift, xs, mod[l], norm_mix_g[l], q_wT_all, i, q_g_cols, fT,
                         sel_q, batch, seq)
            attn = lax.cond(
                preshift_ok,
                functools.partial(_attn_call, batch=batch, seq=seq, preshifted=True),
                functools.partial(_attn_call, batch=batch, seq=seq, preshifted=False),
                qT, k, vT)
            xs, casts = _mlp_call(xs, mod[l], norm_mlp_g[l], *mlp_w_bf16[l], seq,
                                  attn=attn, o_w=o_w_all, o_layer=i, cast_jobs=jobs_for_next(l))
            keep_casts(l, casts)
        if l == n_conv - 1:
            w_f3 = jnp.zeros((d, SLAB), F32).at[:, :3 * N_HEADS].set(jnp.tile(w_f, (1, 3)))
            b_f3 = jnp.zeros((1, SLAB), F32).at[0, :3 * N_HEADS].set(jnp.tile(b_f, 3))
            k, vT, fT = _kv_call(xs, kv_norm_g, w_kv[:, :d].astype(BF16),
                                 w_kv[:, d:].T.astype(BF16), w_f3.astype(BF16), b_f3,
                                 jnp.tile(k_norm_g, 2).reshape(1, SLAB), sel_k, batch, seq)
    return xs.reshape(batch, seq, d)
```

```python
import functools

import numpy as np
import jax
import jax.numpy as jnp
from jax import lax
from jax.experimental import pallas as pl
from jax.experimental.pallas import tpu as pltpu

F32 = jnp.float32
BF16 = jnp.bfloat16

SUBLANES = 8
N_HEADS = 16
HEAD_DIM = 64
SLAB = 2 * HEAD_DIM
V_ROWS = 2 * HEAD_DIM
N_MOD = 6
CONV_W = 3
EPS = 1e-6
MASK_VALUE = -1e30

TOKEN_TILE = 512
Q_TOKEN_TILE = 1024
FF_CHUNK = 1024
Q_TILE = 1024
KV_TILE = 1024
DIAG_BAND = 256
ADA_COLS = 1536
LOG2E = 1.4426950408889634
MAX_PRESHIFT = 40.0
BOUND_SLACK = 1.0 + 2.0 ** -6
V7X_VMEM_BYTES = 64 * 1024 * 1024
VMEM_LIMIT_BYTES = V7X_VMEM_BYTES - 8 * 1024 * 1024

NT_DIMS = (((1,), (1,)), ((), ()))
TN_DIMS = (((0,), (0,)), ((), ()))


def _cparams(n_axes):
    return pltpu.CompilerParams(
        dimension_semantics=("arbitrary",) * n_axes,
        vmem_limit_bytes=VMEM_LIMIT_BYTES)


def _resident(block_shape, index_map):
    return pl.BlockSpec(block_shape, index_map, pipeline_mode=pl.Buffered(1))


def _layer_spec(w, layer=None):
    if w.ndim == 2:
        return _resident(w.shape, lambda *_: (0, 0))
    _, rows, cols = w.shape
    return _resident((None, rows, cols), lambda *_: (layer, 0, 0))


def _with_cast_jobs(body, n_in, n_out, n_jobs):
    def kernel(*refs):
        ins, rest = refs[:n_in], refs[n_in:]
        job_src, rest = rest[:n_jobs], rest[n_jobs:]
        outs, rest = rest[:n_out], rest[n_out:]
        job_dst, scratch = rest[:n_jobs], rest[n_jobs:]
        body(*ins, *outs, *scratch)
        for src, dst in zip(job_src, job_dst):
            dst[...] = src[...].astype(BF16)
    return kernel


def _cast_job_specs(jobs, n_steps):
    in_specs, out_specs, out_shapes, operands = [], [], [], []
    for stack, layer in jobs:
        _, rows, cols = stack.shape
        slab = rows // n_steps
        assert slab * n_steps == rows and slab % 16 == 0
        in_specs.append(pl.BlockSpec((None, slab, cols), lambda t, layer=layer: (layer, t, 0)))
        out_specs.append(pl.BlockSpec((slab, cols), lambda t: (t, 0)))
        out_shapes.append(jax.ShapeDtypeStruct((rows, cols), BF16))
        operands.append(stack)
    return in_specs, out_specs, out_shapes, operands


def _rms_scale(x):
    ms = jnp.mean(x * x, axis=-1, keepdims=True)
    return x * lax.rsqrt(ms + EPS)


def _norm_mod(x, g, shift, scale):
    return (_rms_scale(x) * g) * (1.0 + scale) + shift


def _split3(v):
    hi = v.astype(BF16).astype(F32)
    r = v - hi
    mid = r.astype(BF16).astype(F32)
    lo = (r - mid).astype(BF16).astype(F32)
    return hi, mid, lo


def _ada_kernel(c_ref, w_ref, b_ref, o_ref):
    c = c_ref[...]
    ca = c * jax.nn.sigmoid(c)
    o_ref[0] = jnp.dot(ca.astype(BF16), w_ref[0].astype(BF16),
                       preferred_element_type=F32) + b_ref[0]


def _ada_call(c, ada_w, ada_b):
    depth, d, nm = ada_w.shape
    b = c.shape[0]
    rows = SUBLANES
    c_pad = jnp.zeros((rows, d), F32).at[:b].set(c)
    out = pl.pallas_call(
        _ada_kernel,
        grid=(depth, nm // ADA_COLS),
        in_specs=[
            pl.BlockSpec((rows, d), lambda l, n: (0, 0)),
            pl.BlockSpec((1, d, ADA_COLS), lambda l, n: (l, 0, n)),
            pl.BlockSpec((1, 1, ADA_COLS), lambda l, n: (l, 0, n)),
        ],
        out_specs=pl.BlockSpec((1, rows, ADA_COLS), lambda l, n: (l, 0, n)),
        out_shape=jax.ShapeDtypeStruct((depth, rows, nm), F32),
        compiler_params=_cparams(2),
        name="ada_mod",
    )(c_pad, ada_w, ada_b.reshape(depth, 1, nm))
    return out[:, :b, :].reshape(depth, b, N_MOD, 1, d)


def _conv_kernel(tiles_per_seq, x_ref, mod_ref, g_ref, win_ref, cw_ref, wout_ref,
                 o_ref, zbuf):
    tm, d = x_ref.shape
    t = pl.program_id(0)

    @pl.when(t % tiles_per_seq == 0)
    def _():
        zbuf[0:SUBLANES, :] = jnp.zeros((SUBLANES, d), F32)

    x = x_ref[...]
    h = _norm_mod(x, g_ref[...], mod_ref[0], mod_ref[1]).astype(BF16)
    cx = jnp.dot(h, win_ref[:, d:], preferred_element_type=F32)
    z = cx[:, :d] * cx[:, d:]
    zbuf[SUBLANES:SUBLANES + tm, :] = z
    gate = jnp.dot(h, win_ref[:, :d], preferred_element_type=F32)
    z1 = zbuf[SUBLANES - 1:SUBLANES - 1 + tm, :]
    z2 = zbuf[SUBLANES - 2:SUBLANES - 2 + tm, :]
    zc = cw_ref[0] * z2 + cw_ref[1] * z1 + cw_ref[2] * z
    zbuf[0:SUBLANES, :] = z[tm - SUBLANES:, :]
    gated = (gate * zc).astype(BF16)
    y = jnp.dot(gated, wout_ref[...], preferred_element_type=F32)
    o_ref[...] = x + mod_ref[2] * y


def _conv_call(x, mod_l, g, w_in, conv_w, w_out, seq, cast_jobs=()):
    n, d = x.shape
    tm = TOKEN_TILE
    tps = seq // tm
    steps = n // tm
    job_in, job_out, job_shapes, job_args = _cast_job_specs(cast_jobs, steps)
    in_specs = [
        pl.BlockSpec((tm, d), lambda t: (t, 0)),
        pl.BlockSpec((None, N_MOD, 1, d), lambda t: (t // tps, 0, 0, 0)),
        _resident((1, d), lambda t: (0, 0)),
        _layer_spec(w_in),
        _resident((CONV_W, 1, d), lambda t: (0, 0, 0)),
        _layer_spec(w_out),
    ]
    outs = pl.pallas_call(
        _with_cast_jobs(functools.partial(_conv_kernel, tps), len(in_specs), 1, len(cast_jobs)),
        grid=(steps,),
        in_specs=in_specs + job_in,
        out_specs=[pl.BlockSpec((tm, d), lambda t: (t, 0))] + job_out,
        out_shape=[jax.ShapeDtypeStruct((n, d), F32)] + job_shapes,
        scratch_shapes=[pltpu.VMEM((tm + SUBLANES, d), F32)],
        compiler_params=_cparams(1),
        name="conv_mixer",
    )(x, mod_l, g.reshape(1, d), w_in, conv_w.reshape(CONV_W, 1, d), w_out, *job_args)
    return outs[0], list(outs[1:])


def _mlp_residual(x1, mod_ref, g, w1_ref, w2_ref):
    d, ff = w1_ref.shape
    h = _norm_mod(x1, g, mod_ref[3], mod_ref[4]).astype(BF16)
    acc = jnp.zeros(x1.shape, F32)
    for c in range(ff // FF_CHUNK):
        cols = slice(c * FF_CHUNK, (c + 1) * FF_CHUNK)
        a = jnp.maximum(jnp.dot(h, w1_ref[:, cols], preferred_element_type=F32), 0.0)
        acc = acc + jnp.dot((a * a).astype(BF16), w2_ref[cols, :],
                            preferred_element_type=F32)
    return x1 + mod_ref[5] * acc


def _mlp_kernel(x_ref, mod_ref, g_ref, w1_ref, w2_ref, o_ref):
    o_ref[...] = _mlp_residual(x_ref[...], mod_ref, g_ref[...], w1_ref, w2_ref)


def _omlp_kernel(x_ref, aT_ref, mod_ref, g_ref, ow_ref, w1_ref, w2_ref, o_ref):
    y = lax.dot_general(aT_ref[0], ow_ref[...], TN_DIMS, preferred_element_type=F32)
    x1 = x_ref[...] + mod_ref[2] * y
    o_ref[...] = _mlp_residual(x1, mod_ref, g_ref[...], w1_ref, w2_ref)


def _mlp_call(x, mod_l, g, w1, w2, seq, attn=None, o_w=None, o_layer=None, cast_jobs=()):
    n, d = x.shape
    tm = TOKEN_TILE
    tps = seq // tm
    steps = n // tm
    job_in, job_out, job_shapes, job_args = _cast_job_specs(cast_jobs, steps)
    tile = pl.BlockSpec((tm, d), lambda t: (t, 0))
    mod_spec = pl.BlockSpec((None, N_MOD, 1, d), lambda t: (t // tps, 0, 0, 0))
    g_spec = _resident((1, d), lambda t: (0, 0))
    w_specs = [_layer_spec(w1), _layer_spec(w2)]
    if attn is None:
        body, name = _mlp_kernel, "mlp"
        in_specs = [tile, mod_spec, g_spec] + w_specs
        args = (x, mod_l, g.reshape(1, d), w1, w2)
    else:
        body, name = _omlp_kernel, "oproj_mlp"
        attn_spec = pl.BlockSpec((1, d, tm), lambda t: (t // tps, 0, t % tps))
        in_specs = [tile, attn_spec, mod_spec, g_spec, _layer_spec(o_w, o_layer)] + w_specs
        args = (x, attn, mod_l, g.reshape(1, d), o_w, w1, w2)
    outs = pl.pallas_call(
        _with_cast_jobs(body, len(in_specs), 1, len(cast_jobs)),
        grid=(steps,),
        in_specs=in_specs + job_in,
        out_specs=[tile] + job_out,
        out_shape=[jax.ShapeDtypeStruct((n, d), F32)] + job_shapes,
        compiler_params=_cparams(1),
        name=name,
    )(*args, *job_args)
    return outs[0], list(outs[1:])


def _extra_base(head):
    return HEAD_DIM if head % 2 == 0 else 0


ONE_LANE = SLAB - 1


EXTRA_ROWS = 16
Q_ROWS = HEAD_DIM + EXTRA_ROWS


def _selection_matrices():
    sel_k = np.zeros((SLAB, N_HEADS // 2 * SLAB), np.float32)
    sel_q = np.zeros((N_HEADS * EXTRA_ROWS, SLAB), np.float32)
    for h in range(N_HEADS):
        base = (h // 2) * SLAB + _extra_base(h)
        for piece in range(3):
            sel_q[h * EXTRA_ROWS + piece, piece * N_HEADS + h] = 1.0
            sel_q[h * EXTRA_ROWS + 3 + piece, ONE_LANE] = 1.0
            sel_k[ONE_LANE, base + piece] = 1.0
            sel_k[piece * N_HEADS + h, base + 3 + piece] = -1.0
    return jnp.asarray(sel_k, BF16), jnp.asarray(sel_q, BF16)


def _pieces(f, index):
    hi, mid, lo = _split3(f)
    one = jnp.where(index == ONE_LANE, 1.0, 0.0)
    return jnp.where(index < N_HEADS, hi,
                     jnp.where(index < 2 * N_HEADS, mid,
                               jnp.where(index < 3 * N_HEADS, lo, one))).astype(BF16)


def _kv_kernel(tiles_per_seq, x_ref, g_ref, wk_ref, wvT_ref, wf_ref, bf_ref, kg_ref,
               selk_ref, tri_ref, k_ref, vT_ref, fT_ref, carry):
    tm, d = x_ref.shape
    t = pl.program_id(0)

    @pl.when(t % tiles_per_seq == 0)
    def _():
        carry[...] = jnp.zeros(carry.shape, F32)

    h = (_rms_scale(x_ref[...]) * g_ref[...]).astype(BF16)

    xf = jnp.dot(h, wf_ref[...], preferred_element_type=F32) + bf_ref[...]
    k = jnp.dot(h, wk_ref[...], preferred_element_type=F32)
    log_f = jnp.minimum(xf, 0.0) - jnp.log1p(jnp.exp(-jnp.abs(xf)))
    pieces = jnp.concatenate([p.astype(BF16) for p in _split3(log_f)], axis=-1)
    sums = jnp.dot(tri_ref[...], pieces, preferred_element_type=F32)
    vT = lax.dot_general(wvT_ref[...], h, NT_DIMS, preferred_element_type=F32)
    f = carry[0:1, :] + sums[:, 0:SLAB] + sums[:, SLAB:2 * SLAB] + sums[:, 2 * SLAB:]
    carry[...] = jnp.broadcast_to(f[tm - 1:tm, :], carry.shape)
    f2 = f * LOG2E
    fT_ref[0] = f2.T

    lane = lax.broadcasted_iota(jnp.int32, (tm, SLAB), 1)
    extras = jnp.dot(_pieces(f2, lane), selk_ref[...], preferred_element_type=F32)

    low = lane < HEAD_DIM
    for j in range(N_HEADS // 2):
        ks = k[:, j * SLAB:(j + 1) * SLAB]
        k2 = ks * ks
        ms_lo = jnp.sum(jnp.where(low, k2, 0.0), axis=-1, keepdims=True) * (1.0 / HEAD_DIM)
        ms_hi = jnp.sum(jnp.where(low, 0.0, k2), axis=-1, keepdims=True) * (1.0 / HEAD_DIM)
        kn = ks * jnp.where(low, lax.rsqrt(ms_lo + EPS), lax.rsqrt(ms_hi + EPS)) * kg_ref[...]
        ex = extras[:, j * SLAB:(j + 1) * SLAB]
        k_ref[0, 2 * j] = jnp.where(low, kn, ex).astype(BF16)
        k_ref[0, 2 * j + 1] = jnp.where(low, ex, kn).astype(BF16)

    row = lax.broadcasted_iota(jnp.int32, (V_ROWS - HEAD_DIM, tm), 0)
    ones_blk = jnp.where(row == 0, 1.0, 0.0).astype(BF16)
    for hd in range(N_HEADS):
        vT_ref[0, hd, 0, 0:HEAD_DIM, :] = vT[hd * HEAD_DIM:(hd + 1) * HEAD_DIM, :].astype(BF16)
        vT_ref[0, hd, 0, HEAD_DIM:V_ROWS, :] = ones_blk


def _kv_call(x, g, w_k, w_vT, w_f3, b_f3, k_g2, sel_k, batch, seq):
    n, d = x.shape
    tm = TOKEN_TILE
    tps = seq // tm
    tpk = KV_TILE // tm
    tri = jnp.asarray(np.tril(np.ones((tm, tm), np.float32)), BF16)
    return pl.pallas_call(
        functools.partial(_kv_kernel, tps),
        grid=(n // tm,),
        in_specs=[
            pl.BlockSpec((tm, d), lambda t: (t, 0)),
            _resident((1, d), lambda t: (0, 0)),
            _resident((d, d), lambda t: (0, 0)),
            _resident((d, d), lambda t: (0, 0)),
            _resident((d, SLAB), lambda t: (0, 0)),
            _resident((1, SLAB), lambda t: (0, 0)),
            _resident((1, SLAB), lambda t: (0, 0)),
            _resident((SLAB, N_HEADS // 2 * SLAB), lambda t: (0, 0)),
            _resident((tm, tm), lambda t: (0, 0)),
        ],
        out_specs=[
            pl.BlockSpec((1, N_HEADS, tm, SLAB), lambda t: (t // tps, 0, t % tps, 0)),
            pl.BlockSpec((1, N_HEADS, 1, V_ROWS, tm),
                         lambda t: (t // tps, 0, (t % tps) // tpk, 0, t % tpk)),
            pl.BlockSpec((1, SLAB, tm), lambda t: (t // tps, 0, t % tps)),
        ],
        out_shape=[
            jax.ShapeDtypeStruct((batch, N_HEADS, seq, SLAB), BF16),
            jax.ShapeDtypeStruct((batch, N_HEADS, seq // KV_TILE, V_ROWS, KV_TILE), BF16),
            jax.ShapeDtypeStruct((batch, SLAB, seq), F32),
        ],
        scratch_shapes=[pltpu.VMEM((SUBLANES, SLAB), F32)],
        compiler_params=_cparams(1),
        name="shared_kv",
    )(x, g.reshape(1, d), w_k, w_vT, w_f3, b_f3, k_g2, sel_k, tri)


def _q_kernel(shift_ref, x_ref, mod_ref, g_ref, qwT_ref, qg_ref, fT_ref, selq_ref, o_ref):
    tm, d = x_ref.shape
    row = lax.broadcasted_iota(jnp.int32, (SLAB, tm), 0)
    extras = jnp.dot(selq_ref[...], _pieces(fT_ref[0] - shift_ref[0], row),
                     preferred_element_type=F32)
    h = _norm_mod(x_ref[...], g_ref[...], mod_ref[0], mod_ref[1]).astype(BF16)
    qT = lax.dot_general(qwT_ref[...], h, NT_DIMS, preferred_element_type=F32)
    for hd in range(N_HEADS):
        qh = qT[hd * HEAD_DIM:(hd + 1) * HEAD_DIM, :]
        ms = jnp.mean(qh * qh, axis=0, keepdims=True)
        qn = qh * lax.rsqrt(ms + EPS) * qg_ref[...]
        o_ref[0, hd, 0:HEAD_DIM, :] = qn.astype(BF16)
        o_ref[0, hd, HEAD_DIM:Q_ROWS, :] = (
            extras[hd * EXTRA_ROWS:(hd + 1) * EXTRA_ROWS, :].astype(BF16))


def _q_call(shift, x, mod_l, g, q_wT, layer, q_g_cols, fT, sel_q, batch, seq):
    n, d = x.shape
    tm = Q_TOKEN_TILE
    tps = seq // tm
    return pl.pallas_call(
        _q_kernel,
        grid=(n // tm,),
        in_specs=[
            pl.BlockSpec(memory_space=pltpu.SMEM),
            pl.BlockSpec((tm, d), lambda t: (t, 0)),
            pl.BlockSpec((None, N_MOD, 1, d), lambda t: (t // tps, 0, 0, 0)),
            _resident((1, d), lambda t: (0, 0)),
            _layer_spec(q_wT, layer),
            _resident((HEAD_DIM, tm), lambda t: (0, 0)),
            pl.BlockSpec((1, SLAB, tm), lambda t: (t // tps, 0, t % tps)),
            _resident((N_HEADS * EXTRA_ROWS, SLAB), lambda t: (0, 0)),
        ],
        out_specs=pl.BlockSpec((1, N_HEADS, Q_ROWS, tm), lambda t: (t // tps, 0, 0, t % tps)),
        out_shape=jax.ShapeDtypeStruct((batch, N_HEADS, Q_ROWS, seq), BF16),
        compiler_params=_cparams(1),
        name="fox_query",
    )(shift, x, mod_l, g.reshape(1, d), q_wT, q_g_cols, fT, sel_q)


def _store_normalized(acc_ref, o_ref):
    for hh in range(2):
        inv = 1.0 / acc_ref[hh, HEAD_DIM:HEAD_DIM + 1, :]
        o_ref[0, hh * HEAD_DIM:(hh + 1) * HEAD_DIM, :] = (
            acc_ref[hh, 0:HEAD_DIM, :] * inv).astype(BF16)


def _expand_queries(qc_ref, q_ref):
    tq = q_ref.shape[-1]
    zeros = jnp.zeros((HEAD_DIM - EXTRA_ROWS, tq), BF16)
    q_ref[0, 0:Q_ROWS, :] = qc_ref[0, 0]
    q_ref[0, Q_ROWS:SLAB, :] = zeros
    q_ref[1, 0:EXTRA_ROWS, :] = qc_ref[0, 1, HEAD_DIM:Q_ROWS, :]
    q_ref[1, EXTRA_ROWS:HEAD_DIM, :] = zeros
    q_ref[1, HEAD_DIM:SLAB, :] = qc_ref[0, 1, 0:HEAD_DIM, :]


def _attn_kernel(qc_ref, k_ref, vT_ref, o_ref, acc_ref, m_ref, qT_ref):
    tq = qT_ref.shape[-1]
    tk = vT_ref.shape[-1]
    i = pl.program_id(2)
    n_kv = (i * tq + tq + tk - 1) // tk

    _expand_queries(qc_ref, qT_ref)
    for hh in range(2):
        acc_ref[hh] = jnp.zeros((V_ROWS, tq), F32)
        m_ref[hh] = jnp.full((SUBLANES, tq), MASK_VALUE, F32)

    def step(kb, masked):
        start = pl.multiple_of(kb * tk, tk)
        for hh in range(2):
            s = jnp.dot(k_ref[0, hh, pl.ds(start, tk), :], qT_ref[hh],
                        preferred_element_type=F32)
            if masked:
                key_pos = kb * tk + lax.broadcasted_iota(jnp.int32, (tk, tq), 0)
                q_pos = i * tq + lax.broadcasted_iota(jnp.int32, (tk, tq), 1)
                s = jnp.where(key_pos <= q_pos, s, MASK_VALUE)
            m_old = m_ref[hh, 0:1, :]
            m_new = jnp.maximum(m_old, jnp.max(s, axis=0, keepdims=True))
            alpha = jnp.exp2(m_old - m_new)
            p = jnp.exp2(s - m_new).astype(BF16)
            pv = jnp.dot(vT_ref[0, hh, kb], p, preferred_element_type=F32)
            acc_ref[hh] = alpha * acc_ref[hh] + pv
            m_ref[hh] = jnp.broadcast_to(m_new, (SUBLANES, tq))

    def body(kb, carry):
        step(kb, False)
        return carry

    lax.fori_loop(0, n_kv - 1, body, 0)
    step(n_kv - 1, True)

    _store_normalized(acc_ref, o_ref)


def _attn_preshifted_kernel(qc_ref, k_ref, vT_ref, o_ref, acc_ref, p_ref, qT_ref):
    tq = qT_ref.shape[-1]
    tk = vT_ref.shape[-1]
    assert tq == tk
    i = pl.program_id(2)

    def probabilities(kb, slot):
        start = pl.multiple_of(kb * tk, tk)
        for hh in range(2):
            s = jnp.dot(k_ref[0, hh, pl.ds(start, tk), :], qT_ref[hh],
                        preferred_element_type=F32)
            p_ref[slot, hh] = jnp.exp2(s).astype(BF16)

    def diagonal_probabilities(slot):
        causal = (lax.broadcasted_iota(jnp.int32, (DIAG_BAND, tq), 0)
                  <= lax.broadcasted_iota(jnp.int32, (DIAG_BAND, tq), 1))
        for hh in range(2):
            q = qT_ref[hh]
            for r in range(tk // DIAG_BAND):
                lo = r * DIAG_BAND
                start = pl.multiple_of(i * tk + lo, DIAG_BAND)
                s = jnp.dot(k_ref[0, hh, pl.ds(start, DIAG_BAND), :], q[:, lo:],
                            preferred_element_type=F32)
                p_ref[slot, hh, lo:lo + DIAG_BAND, lo:] = jnp.exp2(
                    jnp.where(causal[:, :tq - lo], s, MASK_VALUE)).astype(BF16)

    def accumulate(kb, slot):
        for hh in range(2):
            acc_ref[hh] += jnp.dot(vT_ref[0, hh, kb], p_ref[slot, hh],
                                   preferred_element_type=F32)

    def accumulate_diagonal(slot):
        for hh in range(2):
            for c in range(tq // DIAG_BAND):
                hi = (c + 1) * DIAG_BAND
                cols = slice(c * DIAG_BAND, hi)
                acc_ref[hh, :, cols] += jnp.dot(vT_ref[0, hh, i, :, 0:hi], p_ref[slot, hh, 0:hi, cols],
                                                preferred_element_type=F32)

    def clear_accumulators():
        for hh in range(2):
            acc_ref[hh] = jnp.zeros((V_ROWS, tq), F32)

    _expand_queries(qc_ref, qT_ref)

    rest = jnp.maximum(i - 1, 0)
    odd = rest % 2

    @pl.when(jnp.logical_and(i >= 1, odd == 0))
    def _():
        clear_accumulators()
        probabilities(0, 0)

    @pl.when(odd == 1)
    def _():
        clear_accumulators()
        probabilities(0, 1)
        probabilities(1, 0)
        accumulate(0, 1)

    def body(t, pending):
        kb = 1 + odd + 2 * t
        accumulate(pending, 0)
        probabilities(kb, 1)
        probabilities(kb + 1, 0)
        accumulate(kb, 1)
        return kb + 1

    pending = lax.fori_loop(0, rest // 2, body, odd)

    @pl.when(i >= 1)
    def _():
        diagonal_probabilities(1)
        accumulate(pending, 0)
        accumulate_diagonal(1)

    @pl.when(i == 0)
    def _():
        clear_accumulators()
        diagonal_probabilities(1)
        accumulate_diagonal(1)

    _store_normalized(acc_ref, o_ref)


def _attn_call(qT, k, vT, batch, seq, preshifted):
    nkb = seq // KV_TILE
    pairs = N_HEADS // 2
    if preshifted:
        body = _attn_preshifted_kernel
        scratch = [pltpu.VMEM((2, V_ROWS, Q_TILE), F32), pltpu.VMEM((2, 2, KV_TILE, Q_TILE), BF16)]
    else:
        body = _attn_kernel
        scratch = [pltpu.VMEM((2, V_ROWS, Q_TILE), F32), pltpu.VMEM((2, SUBLANES, Q_TILE), F32)]
    scratch.append(pltpu.VMEM((2, SLAB, Q_TILE), BF16))
    return pl.pallas_call(
        body,
        grid=(batch, pairs, seq // Q_TILE),
        in_specs=[
            pl.BlockSpec((1, 2, Q_ROWS, Q_TILE), lambda b, j, i: (b, j, 0, i)),
            pl.BlockSpec((1, 2, seq, SLAB), lambda b, j, i: (b, j, 0, 0)),
            pl.BlockSpec((1, 2, nkb, V_ROWS, KV_TILE), lambda b, j, i: (b, j, 0, 0, 0)),
        ],
        out_specs=pl.BlockSpec((1, SLAB, Q_TILE), lambda b, j, i: (b, j, i)),
        out_shape=jax.ShapeDtypeStruct((batch, N_HEADS * HEAD_DIM, seq), BF16),
        scratch_shapes=scratch,
        compiler_params=_cparams(3),
        name="fox_attention_preshifted" if preshifted else "fox_attention",
    )(qT, k, vT)


def kernel(x, c, ada_w, ada_b, norm_mix_g, norm_mlp_g, sc_w_in, sc_conv, sc_w_out,
           kv_norm_g, w_kv, k_norm_g, w_f, b_f, q_w, q_norm_g, o_w, mlp_w1, mlp_w2):
    batch, seq, d = x.shape
    depth = ada_w.shape[0]
    n_conv = sc_w_in.shape[0]
    assert d == N_HEADS * HEAD_DIM and seq % KV_TILE == 0 and seq % Q_TOKEN_TILE == 0
    assert KV_TILE % TOKEN_TILE == 0 and KV_TILE == Q_TILE

    mod = _ada_call(c, ada_w, ada_b)
    sel_k, sel_q = _selection_matrices()
    q_scale = LOG2E / np.sqrt(HEAD_DIM)

    q_wT_all, o_w_all = jnp.swapaxes(q_w, 1, 2).astype(BF16), o_w.astype(BF16)
    conv_w_bf16 = {0: [sc_w_in[0].astype(BF16), sc_w_out[0].astype(BF16)]} if n_conv else {}
    mlp_w_bf16 = {} if n_conv else {0: [mlp_w1[0].astype(BF16), mlp_w2[0].astype(BF16)]}

    def jobs_for_next(l):
        if l + 1 >= depth:
            return []
        return [(sc_w_in, l + 1), (sc_w_out, l + 1)] if l + 1 < n_conv \
            else [(mlp_w1, l + 1), (mlp_w2, l + 1)]

    def keep_casts(l, casts):
        if casts:
            (conv_w_bf16 if l + 1 < n_conv else mlp_w_bf16)[l + 1] = casts

    xs = x.reshape(batch * seq, d)
    k = vT = fT = None
    for l in range(depth):
        if l < n_conv:
            xs, mlp_w_bf16[l] = _conv_call(xs, mod[l], norm_mix_g[l], conv_w_bf16[l][0], sc_conv[l],
                                           conv_w_bf16[l][1], seq,
                                           cast_jobs=[(mlp_w1, l), (mlp_w2, l)])
            xs, casts = _mlp_call(xs, mod[l], norm_mlp_g[l], *mlp_w_bf16[l], seq,
                                  cast_jobs=jobs_for_next(l))
            keep_casts(l, casts)
        else:
            i = l - n_conv
            q_g_cols = jnp.broadcast_to((q_norm_g[i] * q_scale)[:, None], (HEAD_DIM, Q_TOKEN_TILE))
            bound = (np.sqrt(HEAD_DIM) * BOUND_SLACK) * jnp.max(jnp.abs(q_norm_g[i])) \
                * jnp.max(jnp.abs(k_norm_g))
            preshift_ok = bound <= MAX_PRESHIFT
            shift = jnp.where(preshift_ok, bound * LOG2E, 0.0).reshape(1).astype(F32)
            qT = _q_call(shift, xs, mod[l], norm_mix_g[l], q_wT_all, i, q_g_cols, fT,
                         sel_q, batch, seq)
            attn = lax.cond(
                preshift_ok,
                functools.partial(_attn_call, batch=batch, seq=seq, preshifted=True),
                functools.partial(_attn_call, batch=batch, seq=seq, preshifted=False),
                qT, k, vT)
            xs, casts = _mlp_call(xs, mod[l], norm_mlp_g[l], *mlp_w_bf16[l], seq,
                                  attn=attn, o_w=o_w_all, o_layer=i, cast_jobs=jobs_for_next(l))
            keep_casts(l, casts)
        if l == n_conv - 1:
            w_f3 = jnp.zeros((d, SLAB), F32).at[:, :3 * N_HEADS].set(jnp.tile(w_f, (1, 3)))
            b_f3 = jnp.zeros((1, SLAB), F32).at[0, :3 * N_HEADS].set(jnp.tile(b_f, 3))
            k, vT, fT = _kv_call(xs, kv_norm_g, w_kv[:, :d].astype(BF16),
                                 w_kv[:, d:].T.astype(BF16), w_f3.astype(BF16), b_f3,
                                 jnp.tile(k_norm_g, 2).reshape(1, SLAB), sel_k, batch, seq)
    return xs.reshape(batch, seq, d)
```

```python
import functools

import numpy as np
import jax
import jax.numpy as jnp
from jax import lax
from jax.experimental import pallas as pl
from jax.experimental.pallas import tpu as pltpu

F32 = jnp.float32
BF16 = jnp.bfloat16

SUBLANES = 8
N_HEADS = 16
HEAD_DIM = 64
SLAB = 2 * HEAD_DIM
V_ROWS = 2 * HEAD_DIM
N_MOD = 6
CONV_W = 3
EPS = 1e-6
MASK_VALUE = -1e30

TOKEN_TILE = 512
Q_TOKEN_TILE = 1024
FF_CHUNK = 1024
Q_TILE = 1024
KV_TILE = 1024
DIAG_BAND = 256
ADA_COLS = 1536
LOG2E = 1.4426950408889634
MAX_PRESHIFT = 40.0
BOUND_SLACK = 1.0 + 2.0 ** -6
V7X_VMEM_BYTES = 64 * 1024 * 1024
VMEM_LIMIT_BYTES = V7X_VMEM_BYTES - 8 * 1024 * 1024

NT_DIMS = (((1,), (1,)), ((), ()))
TN_DIMS = (((0,), (0,)), ((), ()))


def _cparams(n_axes):
    return pltpu.CompilerParams(
        dimension_semantics=("arbitrary",) * n_axes,
        vmem_limit_bytes=VMEM_LIMIT_BYTES)


def _resident(block_shape, index_map):
    return pl.BlockSpec(block_shape, index_map, pipeline_mode=pl.Buffered(1))


def _layer_spec(w, layer=None):
    if w.ndim == 2:
        return _resident(w.shape, lambda *_: (0, 0))
    _, rows, cols = w.shape
    return _resident((None, rows, cols), lambda *_: (layer, 0, 0))


def _with_cast_jobs(body, n_in, n_out, n_jobs):
    def kernel(*refs):
        ins, rest = refs[:n_in], refs[n_in:]
        job_src, rest = rest[:n_jobs], rest[n_jobs:]
        outs, rest = rest[:n_out], rest[n_out:]
        job_dst, scratch = rest[:n_jobs], rest[n_jobs:]
        body(*ins, *outs, *scratch)
        for src, dst in zip(job_src, job_dst):
            dst[...] = src[...].astype(BF16)
    return kernel


def _cast_job_specs(jobs, n_steps):
    in_specs, out_specs, out_shapes, operands = [], [], [], []
    for stack, layer in jobs:
        _, rows, cols = stack.shape
        slab = rows // n_steps
        assert slab * n_steps == rows and slab % 16 == 0
        in_specs.append(pl.BlockSpec((None, slab, cols), lambda t, layer=layer: (layer, t, 0)))
        out_specs.append(pl.BlockSpec((slab, cols), lambda t: (t, 0)))
        out_shapes.append(jax.ShapeDtypeStruct((rows, cols), BF16))
        operands.append(stack)
    return in_specs, out_specs, out_shapes, operands


def _rms_scale(x):
    ms = jnp.mean(x * x, axis=-1, keepdims=True)
    return x * lax.rsqrt(ms + EPS)


def _norm_mod(x, g, shift, scale):
    return (_rms_scale(x) * g) * (1.0 + scale) + shift


def _split3(v):
    hi = v.astype(BF16).astype(F32)
    r = v - hi
    mid = r.astype(BF16).astype(F32)
    lo = (r - mid).astype(BF16).astype(F32)
    return hi, mid, lo


def _ada_kernel(c_ref, w_ref, b_ref, o_ref):
    c = c_ref[...]
    ca = c * jax.nn.sigmoid(c)
    o_ref[0] = jnp.dot(ca.astype(BF16), w_ref[0].astype(BF16),
                       preferred_element_type=F32) + b_ref[0]


def _ada_call(c, ada_w, ada_b):
    depth, d, nm = ada_w.shape
    b = c.shape[0]
    rows = SUBLANES
    c_pad = jnp.zeros((rows, d), F32).at[:b].set(c)
    out = pl.pallas_call(
        _ada_kernel,
        grid=(depth, nm // ADA_COLS),
        in_specs=[
            pl.BlockSpec((rows, d), lambda l, n: (0, 0)),
            pl.BlockSpec((1, d, ADA_COLS), lambda l, n: (l, 0, n)),
            pl.BlockSpec((1, 1, ADA_COLS), lambda l, n: (l, 0, n)),
        ],
        out_specs=pl.BlockSpec((1, rows, ADA_COLS), lambda l, n: (l, 0, n)),
        out_shape=jax.ShapeDtypeStruct((depth, rows, nm), F32),
        compiler_params=_cparams(2),
        name="ada_mod",
    )(c_pad, ada_w, ada_b.reshape(depth, 1, nm))
    return out[:, :b, :].reshape(depth, b, N_MOD, 1, d)


def _conv_kernel(tiles_per_seq, x_ref, mod_ref, g_ref, win_ref, cw_ref, wout_ref,
                 o_ref, zbuf):
    tm, d = x_ref.shape
    t = pl.program_id(0)

    @pl.when(t % tiles_per_seq == 0)
    def _():
        zbuf[0:SUBLANES, :] = jnp.zeros((SUBLANES, d), F32)

    x = x_ref[...]
    h = _norm_mod(x, g_ref[...], mod_ref[0], mod_ref[1]).astype(BF16)
    cx = jnp.dot(h, win_ref[:, d:], preferred_element_type=F32)
    z = cx[:, :d] * cx[:, d:]
    zbuf[SUBLANES:SUBLANES + tm, :] = z
    gate = jnp.dot(h, win_ref[:, :d], preferred_element_type=F32)
    z1 = zbuf[SUBLANES - 1:SUBLANES - 1 + tm, :]
    z2 = zbuf[SUBLANES - 2:SUBLANES - 2 + tm, :]
    zc = cw_ref[0] * z2 + cw_ref[1] * z1 + cw_ref[2] * z
    zbuf[0:SUBLANES, :] = z[tm - SUBLANES:, :]
    gated = (gate * zc).astype(BF16)
    y = jnp.dot(gated, wout_ref[...], preferred_element_type=F32)
    o_ref[...] = x + mod_ref[2] * y


def _conv_call(x, mod_l, g, w_in, conv_w, w_out, seq, cast_jobs=()):
    n, d = x.shape
    tm = TOKEN_TILE
    tps = seq // tm
    steps = n // tm
    job_in, job_out, job_shapes, job_args = _cast_job_specs(cast_jobs, steps)
    in_specs = [
        pl.BlockSpec((tm, d), lambda t: (t, 0)),
        pl.BlockSpec((None, N_MOD, 1, d), lambda t: (t // tps, 0, 0, 0)),
        _resident((1, d), lambda t: (0, 0)),
        _layer_spec(w_in),
        _resident((CONV_W, 1, d), lambda t: (0, 0, 0)),
        _layer_spec(w_out),
    ]
    outs = pl.pallas_call(
        _with_cast_jobs(functools.partial(_conv_kernel, tps), len(in_specs), 1, len(cast_jobs)),
        grid=(steps,),
        in_specs=in_specs + job_in,
        out_specs=[pl.BlockSpec((tm, d), lambda t: (t, 0))] + job_out,
        out_shape=[jax.ShapeDtypeStruct((n, d), F32)] + job_shapes,
        scratch_shapes=[pltpu.VMEM((tm + SUBLANES, d), F32)],
        compiler_params=_cparams(1),
        name="conv_mixer",
    )(x, mod_l, g.reshape(1, d), w_in, conv_w.reshape(CONV_W, 1, d), w_out, *job_args)
    return outs[0], list(outs[1:])


def _mlp_residual(x1, mod_ref, g, w1_ref, w2_ref):
    d, ff = w1_ref.shape
    h = _norm_mod(x1, g, mod_ref[3], mod_ref[4]).astype(BF16)
    acc = jnp.zeros(x1.shape, F32)
    for c in range(ff // FF_CHUNK):
        cols = slice(c * FF_CHUNK, (c + 1) * FF_CHUNK)
        a = jnp.maximum(jnp.dot(h, w1_ref[:, cols], preferred_element_type=F32), 0.0)
        acc = acc + jnp.dot((a * a).astype(BF16), w2_ref[cols, :],
                            preferred_element_type=F32)
    return x1 + mod_ref[5] * acc


def _mlp_kernel(x_ref, mod_ref, g_ref, w1_ref, w2_ref, o_ref):
    o_ref[...] = _mlp_residual(x_ref[...], mod_ref, g_ref[...], w1_ref, w2_ref)


def _omlp_kernel(x_ref, aT_ref, mod_ref, g_ref, ow_ref, w1_ref, w2_ref, o_ref):
    half = x_ref.shape[0] // 2
    x1 = []
    for r in range(2):
        rows = slice(r * half, (r + 1) * half)
        y = lax.dot_general(aT_ref[0, :, rows], ow_ref[...], TN_DIMS, preferred_element_type=F32)
        x1.append(x_ref[rows, :] + mod_ref[2] * y)
    for r in range(2):
        o_ref[r * half:(r + 1) * half, :] = _mlp_residual(x1[r], mod_ref, g_ref[...], w1_ref, w2_ref)


def _mlp_call(x, mod_l, g, w1, w2, seq, attn=None, o_w=None, o_layer=None, cast_jobs=()):
    n, d = x.shape
    tm = TOKEN_TILE
    tps = seq // tm
    steps = n // tm
    job_in, job_out, job_shapes, job_args = _cast_job_specs(cast_jobs, steps)
    tile = pl.BlockSpec((tm, d), lambda t: (t, 0))
    mod_spec = pl.BlockSpec((None, N_MOD, 1, d), lambda t: (t // tps, 0, 0, 0))
    g_spec = _resident((1, d), lambda t: (0, 0))
    w_specs = [_layer_spec(w1), _layer_spec(w2)]
    if attn is None:
        body, name = _mlp_kernel, "mlp"
        in_specs = [tile, mod_spec, g_spec] + w_specs
        args = (x, mod_l, g.reshape(1, d), w1, w2)
    else:
        body, name = _omlp_kernel, "oproj_mlp"
        attn_spec = pl.BlockSpec((1, d, tm), lambda t: (t // tps, 0, t % tps))
        in_specs = [tile, attn_spec, mod_spec, g_spec, _layer_spec(o_w, o_layer)] + w_specs
        args = (x, attn, mod_l, g.reshape(1, d), o_w, w1, w2)
    outs = pl.pallas_call(
        _with_cast_jobs(body, len(in_specs), 1, len(cast_jobs)),
        grid=(steps,),
        in_specs=in_specs + job_in,
        out_specs=[tile] + job_out,
        out_shape=[jax.ShapeDtypeStruct((n, d), F32)] + job_shapes,
        compiler_params=_cparams(1),
        name=name,
    )(*args, *job_args)
    return outs[0], list(outs[1:])


def _extra_base(head):
    return HEAD_DIM if head % 2 == 0 else 0


ONE_LANE = SLAB - 1


EXTRA_ROWS = 16
Q_ROWS = HEAD_DIM + EXTRA_ROWS


def _selection_matrices():
    sel_k = np.zeros((SLAB, N_HEADS // 2 * SLAB), np.float32)
    sel_q = np.zeros((N_HEADS * EXTRA_ROWS, SLAB), np.float32)
    for h in range(N_HEADS):
        base = (h // 2) * SLAB + _extra_base(h)
        for piece in range(3):
            sel_q[h * EXTRA_ROWS + piece, piece * N_HEADS + h] = 1.0
            sel_q[h * EXTRA_ROWS + 3 + piece, ONE_LANE] = 1.0
            sel_k[ONE_LANE, base + piece] = 1.0
            sel_k[piece * N_HEADS + h, base + 3 + piece] = -1.0
    return jnp.asarray(sel_k, BF16), jnp.asarray(sel_q, BF16)


def _pieces(f, index):
    hi, mid, lo = _split3(f)
    one = jnp.where(index == ONE_LANE, 1.0, 0.0)
    return jnp.where(index < N_HEADS, hi,
                     jnp.where(index < 2 * N_HEADS, mid,
                               jnp.where(index < 3 * N_HEADS, lo, one))).astype(BF16)


def _kv_kernel(tiles_per_seq, x_ref, g_ref, wk_ref, wvT_ref, wf_ref, bf_ref, kg_ref,
               selk_ref, tri_ref, k_ref, vT_ref, fT_ref, carry):
    tm, d = x_ref.shape
    t = pl.program_id(0)

    @pl.when(t % tiles_per_seq == 0)
    def _():
        carry[...] = jnp.zeros(carry.shape, F32)

    h = (_rms_scale(x_ref[...]) * g_ref[...]).astype(BF16)

    xf = jnp.dot(h, wf_ref[...], preferred_element_type=F32) + bf_ref[...]
    k = jnp.dot(h, wk_ref[...], preferred_element_type=F32)
    log_f = jnp.minimum(xf, 0.0) - jnp.log1p(jnp.exp(-jnp.abs(xf)))
    pieces = jnp.concatenate([p.astype(BF16) for p in _split3(log_f)], axis=-1)
    sums = jnp.dot(tri_ref[...], pieces, preferred_element_type=F32)
    vT = lax.dot_general(wvT_ref[...], h, NT_DIMS, preferred_element_type=F32)
    f = carry[0:1, :] + sums[:, 0:SLAB] + sums[:, SLAB:2 * SLAB] + sums[:, 2 * SLAB:]
    carry[...] = jnp.broadcast_to(f[tm - 1:tm, :], carry.shape)
    f2 = f * LOG2E
    fT_ref[0] = f2.T

    lane = lax.broadcasted_iota(jnp.int32, (tm, SLAB), 1)
    extras = jnp.dot(_pieces(f2, lane), selk_ref[...], preferred_element_type=F32)

    low = lane < HEAD_DIM
    for j in range(N_HEADS // 2):
        ks = k[:, j * SLAB:(j + 1) * SLAB]
        k2 = ks * ks
        ms_lo = jnp.sum(jnp.where(low, k2, 0.0), axis=-1, keepdims=True) * (1.0 / HEAD_DIM)
        ms_hi = jnp.sum(jnp.where(low, 0.0, k2), axis=-1, keepdims=True) * (1.0 / HEAD_DIM)
        kn = ks * jnp.where(low, lax.rsqrt(ms_lo + EPS), lax.rsqrt(ms_hi + EPS)) * kg_ref[...]
        ex = extras[:, j * SLAB:(j + 1) * SLAB]
        k_ref[0, 2 * j] = jnp.where(low, kn, ex).astype(BF16)
        k_ref[0, 2 * j + 1] = jnp.where(low, ex, kn).astype(BF16)

    row = lax.broadcasted_iota(jnp.int32, (V_ROWS - HEAD_DIM, tm), 0)
    ones_blk = jnp.where(row == 0, 1.0, 0.0).astype(BF16)
    for hd in range(N_HEADS):
        vT_ref[0, hd, 0, 0:HEAD_DIM, :] = vT[hd * HEAD_DIM:(hd + 1) * HEAD_DIM, :].astype(BF16)
        vT_ref[0, hd, 0, HEAD_DIM:V_ROWS, :] = ones_blk


def _kv_call(x, g, w_k, w_vT, w_f3, b_f3, k_g2, sel_k, batch, seq):
    n, d = x.shape
    tm = TOKEN_TILE
    tps = seq // tm
    tpk = KV_TILE // tm
    tri = jnp.asarray(np.tril(np.ones((tm, tm), np.float32)), BF16)
    return pl.pallas_call(
        functools.partial(_kv_kernel, tps),
        grid=(n // tm,),
        in_specs=[
            pl.BlockSpec((tm, d), lambda t: (t, 0)),
            _resident((1, d), lambda t: (0, 0)),
            _resident((d, d), lambda t: (0, 0)),
            _resident((d, d), lambda t: (0, 0)),
            _resident((d, SLAB), lambda t: (0, 0)),
            _resident((1, SLAB), lambda t: (0, 0)),
            _resident((1, SLAB), lambda t: (0, 0)),
            _resident((SLAB, N_HEADS // 2 * SLAB), lambda t: (0, 0)),
            _resident((tm, tm), lambda t: (0, 0)),
        ],
        out_specs=[
            pl.BlockSpec((1, N_HEADS, tm, SLAB), lambda t: (t // tps, 0, t % tps, 0)),
            pl.BlockSpec((1, N_HEADS, 1, V_ROWS, tm),
                         lambda t: (t // tps, 0, (t % tps) // tpk, 0, t % tpk)),
            pl.BlockSpec((1, SLAB, tm), lambda t: (t // tps, 0, t % tps)),
        ],
        out_shape=[
            jax.ShapeDtypeStruct((batch, N_HEADS, seq, SLAB), BF16),
            jax.ShapeDtypeStruct((batch, N_HEADS, seq // KV_TILE, V_ROWS, KV_TILE), BF16),
            jax.ShapeDtypeStruct((batch, SLAB, seq), F32),
        ],
        scratch_shapes=[pltpu.VMEM((SUBLANES, SLAB), F32)],
        compiler_params=_cparams(1),
        name="shared_kv",
    )(x, g.reshape(1, d), w_k, w_vT, w_f3, b_f3, k_g2, sel_k, tri)


def _q_kernel(shift_ref, x_ref, mod_ref, g_ref, qwT_ref, qg_ref, fT_ref, selq_ref, o_ref):
    tm, d = x_ref.shape
    row = lax.broadcasted_iota(jnp.int32, (SLAB, tm), 0)
    extras = jnp.dot(selq_ref[...], _pieces(fT_ref[0] - shift_ref[0], row),
                     preferred_element_type=F32)
    h = _norm_mod(x_ref[...], g_ref[...], mod_ref[0], mod_ref[1]).astype(BF16)
    qT = lax.dot_general(qwT_ref[...], h, NT_DIMS, preferred_element_type=F32)
    for hd in range(N_HEADS):
        qh = qT[hd * HEAD_DIM:(hd + 1) * HEAD_DIM, :]
        ms = jnp.mean(qh * qh, axis=0, keepdims=True)
        qn = qh * lax.rsqrt(ms + EPS) * qg_ref[...]
        o_ref[0, hd, 0:HEAD_DIM, :] = qn.astype(BF16)
        o_ref[0, hd, HEAD_DIM:Q_ROWS, :] = (
            extras[hd * EXTRA_ROWS:(hd + 1) * EXTRA_ROWS, :].astype(BF16))


def _q_call(shift, x, mod_l, g, q_wT, layer, q_g_cols, fT, sel_q, batch, seq):
    n, d = x.shape
    tm = Q_TOKEN_TILE
    tps = seq // tm
    return pl.pallas_call(
        _q_kernel,
        grid=(n // tm,),
        in_specs=[
            pl.BlockSpec(memory_space=pltpu.SMEM),
            pl.BlockSpec((tm, d), lambda t: (t, 0)),
            pl.BlockSpec((None, N_MOD, 1, d), lambda t: (t // tps, 0, 0, 0)),
            _resident((1, d), lambda t: (0, 0)),
            _layer_spec(q_wT, layer),
            _resident((HEAD_DIM, tm), lambda t: (0, 0)),
            pl.BlockSpec((1, SLAB, tm), lambda t: (t // tps, 0, t % tps)),
            _resident((N_HEADS * EXTRA_ROWS, SLAB), lambda t: (0, 0)),
        ],
        out_specs=pl.BlockSpec((1, N_HEADS, Q_ROWS, tm), lambda t: (t // tps, 0, 0, t % tps)),
        out_shape=jax.ShapeDtypeStruct((batch, N_HEADS, Q_ROWS, seq), BF16),
        compiler_params=_cparams(1),
        name="fox_query",
    )(shift, x, mod_l, g.reshape(1, d), q_wT, q_g_cols, fT, sel_q)


def _store_normalized(acc_ref, o_ref):
    for hh in range(2):
        inv = 1.0 / acc_ref[hh, HEAD_DIM:HEAD_DIM + 1, :]
        o_ref[0, hh * HEAD_DIM:(hh + 1) * HEAD_DIM, :] = (
            acc_ref[hh, 0:HEAD_DIM, :] * inv).astype(BF16)


def _expand_queries(qc_ref, q_ref):
    tq = q_ref.shape[-1]
    zeros = jnp.zeros((HEAD_DIM - EXTRA_ROWS, tq), BF16)
    q_ref[0, 0:Q_ROWS, :] = qc_ref[0, 0]
    q_ref[0, Q_ROWS:SLAB, :] = zeros
    q_ref[1, 0:EXTRA_ROWS, :] = qc_ref[0, 1, HEAD_DIM:Q_ROWS, :]
    q_ref[1, EXTRA_ROWS:HEAD_DIM, :] = zeros
    q_ref[1, HEAD_DIM:SLAB, :] = qc_ref[0, 1, 0:HEAD_DIM, :]


def _attn_kernel(qc_ref, k_ref, vT_ref, o_ref, acc_ref, m_ref, qT_ref):
    tq = qT_ref.shape[-1]
    tk = vT_ref.shape[-1]
    i = pl.program_id(2)
    n_kv = (i * tq + tq + tk - 1) // tk

    _expand_queries(qc_ref, qT_ref)
    for hh in range(2):
        acc_ref[hh] = jnp.zeros((V_ROWS, tq), F32)
        m_ref[hh] = jnp.full((SUBLANES, tq), MASK_VALUE, F32)

    def step(kb, masked):
        start = pl.multiple_of(kb * tk, tk)
        for hh in range(2):
            s = jnp.dot(k_ref[0, hh, pl.ds(start, tk), :], qT_ref[hh],
                        preferred_element_type=F32)
            if masked:
                key_pos = kb * tk + lax.broadcasted_iota(jnp.int32, (tk, tq), 0)
                q_pos = i * tq + lax.broadcasted_iota(jnp.int32, (tk, tq), 1)
                s = jnp.where(key_pos <= q_pos, s, MASK_VALUE)
            m_old = m_ref[hh, 0:1, :]
            m_new = jnp.maximum(m_old, jnp.max(s, axis=0, keepdims=True))
            alpha = jnp.exp2(m_old - m_new)
            p = jnp.exp2(s - m_new).astype(BF16)
            pv = jnp.dot(vT_ref[0, hh, kb], p, preferred_element_type=F32)
            acc_ref[hh] = alpha * acc_ref[hh] + pv
            m_ref[hh] = jnp.broadcast_to(m_new, (SUBLANES, tq))

    def body(kb, carry):
        step(kb, False)
        return carry

    lax.fori_loop(0, n_kv - 1, body, 0)
    step(n_kv - 1, True)

    _store_normalized(acc_ref, o_ref)


def _attn_preshifted_kernel(qc_ref, k_ref, vT_ref, o_ref, acc_ref, p_ref, qT_ref):
    tq = qT_ref.shape[-1]
    tk = vT_ref.shape[-1]
    assert tq == tk
    i = pl.program_id(2)

    def probabilities(kb, slot):
        start = pl.multiple_of(kb * tk, tk)
        for hh in range(2):
            s = jnp.dot(k_ref[0, hh, pl.ds(start, tk), :], qT_ref[hh],
                        preferred_element_type=F32)
            p_ref[slot, hh] = jnp.exp2(s).astype(BF16)

    def diagonal_probabilities(slot):
        causal = (lax.broadcasted_iota(jnp.int32, (DIAG_BAND, tq), 0)
                  <= lax.broadcasted_iota(jnp.int32, (DIAG_BAND, tq), 1))
        for hh in range(2):
            q = qT_ref[hh]
            for r in range(tk // DIAG_BAND):
                lo = r * DIAG_BAND
                start = pl.multiple_of(i * tk + lo, DIAG_BAND)
                s = jnp.dot(k_ref[0, hh, pl.ds(start, DIAG_BAND), :], q[:, lo:],
                            preferred_element_type=F32)
                p_ref[slot, hh, lo:lo + DIAG_BAND, lo:] = jnp.exp2(
                    jnp.where(causal[:, :tq - lo], s, MASK_VALUE)).astype(BF16)

    def accumulate(kb, slot):
        for hh in range(2):
            acc_ref[hh] += jnp.dot(vT_ref[0, hh, kb], p_ref[slot, hh],
                                   preferred_element_type=F32)

    def accumulate_diagonal(slot):
        for hh in range(2):
            for c in range(tq // DIAG_BAND):
                hi = (c + 1) * DIAG_BAND
                cols = slice(c * DIAG_BAND, hi)
                acc_ref[hh, :, cols] += jnp.dot(vT_ref[0, hh, i, :, 0:hi], p_ref[slot, hh, 0:hi, cols],
                                                preferred_element_type=F32)

    def clear_accumulators():
        for hh in range(2):
            acc_ref[hh] = jnp.zeros((V_ROWS, tq), F32)

    _expand_queries(qc_ref, qT_ref)

    rest = jnp.maximum(i - 1, 0)
    odd = rest % 2

    @pl.when(jnp.logical_and(i >= 1, odd == 0))
    def _():
        clear_accumulators()
        probabilities(0, 0)

    @pl.when(odd == 1)
    def _():
        clear_accumulators()
        probabilities(0, 1)
        probabilities(1, 0)
        accumulate(0, 1)

    def body(t, pending):
        kb = 1 + odd + 2 * t
        accumulate(pending, 0)
        probabilities(kb, 1)
        probabilities(kb + 1, 0)
        accumulate(kb, 1)
        return kb + 1

    pending = lax.fori_loop(0, rest // 2, body, odd)

    @pl.when(i >= 1)
    def _():
        diagonal_probabilities(1)
        accumulate(pending, 0)
        accumulate_diagonal(1)

    @pl.when(i == 0)
    def _():
        clear_accumulators()
        diagonal_probabilities(1)
        accumulate_diagonal(1)

    _store_normalized(acc_ref, o_ref)


def _attn_call(qT, k, vT, batch, seq, preshifted):
    nkb = seq // KV_TILE
    pairs = N_HEADS // 2
    if preshifted:
        body = _attn_preshifted_kernel
        scratch = [pltpu.VMEM((2, V_ROWS, Q_TILE), F32), pltpu.VMEM((2, 2, KV_TILE, Q_TILE), BF16)]
    else:
        body = _attn_kernel
        scratch = [pltpu.VMEM((2, V_ROWS, Q_TILE), F32), pltpu.VMEM((2, SUBLANES, Q_TILE), F32)]
    scratch.append(pltpu.VMEM((2, SLAB, Q_TILE), BF16))
    return pl.pallas_call(
        body,
        grid=(batch, pairs, seq // Q_TILE),
        in_specs=[
            pl.BlockSpec((1, 2, Q_ROWS, Q_TILE), lambda b, j, i: (b, j, 0, i)),
            pl.BlockSpec((1, 2, seq, SLAB), lambda b, j, i: (b, j, 0, 0)),
            pl.BlockSpec((1, 2, nkb, V_ROWS, KV_TILE), lambda b, j, i: (b, j, 0, 0, 0)),
        ],
        out_specs=pl.BlockSpec((1, SLAB, Q_TILE), lambda b, j, i: (b, j, i)),
        out_shape=jax.ShapeDtypeStruct((batch, N_HEADS * HEAD_DIM, seq), BF16),
        scratch_shapes=scratch,
        compiler_params=_cparams(3),
        name="fox_attention_preshifted" if preshifted else "fox_attention",
    )(qT, k, vT)


def kernel(x, c, ada_w, ada_b, norm_mix_g, norm_mlp_g, sc_w_in, sc_conv, sc_w_out,
           kv_norm_g, w_kv, k_norm_g, w_f, b_f, q_w, q_norm_g, o_w, mlp_w1, mlp_w2):
    batch, seq, d = x.shape
    depth = ada_w.shape[0]
    n_conv = sc_w_in.shape[0]
    assert d == N_HEADS * HEAD_DIM and seq % KV_TILE == 0 and seq % Q_TOKEN_TILE == 0
    assert KV_TILE % TOKEN_TILE == 0 and KV_TILE == Q_TILE

    mod = _ada_call(c, ada_w, ada_b)
    sel_k, sel_q = _selection_matrices()
    q_scale = LOG2E / np.sqrt(HEAD_DIM)

    q_wT_all, o_w_all = jnp.swapaxes(q_w, 1, 2).astype(BF16), o_w.astype(BF16)
    conv_w_bf16 = {0: [sc_w_in[0].astype(BF16), sc_w_out[0].astype(BF16)]} if n_conv else {}
    mlp_w_bf16 = {} if n_conv else {0: [mlp_w1[0].astype(BF16), mlp_w2[0].astype(BF16)]}

    def jobs_for_next(l):
        if l + 1 >= depth:
            return []
        return [(sc_w_in, l + 1), (sc_w_out, l + 1)] if l + 1 < n_conv \
            else [(mlp_w1, l + 1), (mlp_w2, l + 1)]

    def keep_casts(l, casts):
        if casts:
            (conv_w_bf16 if l + 1 < n_conv else mlp_w_bf16)[l + 1] = casts

    xs = x.reshape(batch * seq, d)
    k = vT = fT = None
    for l in range(depth):
        if l < n_conv:
            xs, mlp_w_bf16[l] = _conv_call(xs, mod[l], norm_mix_g[l], conv_w_bf16[l][0], sc_conv[l],
                                           conv_w_bf16[l][1], seq,
                                           cast_jobs=[(mlp_w1, l), (mlp_w2, l)])
            xs, casts = _mlp_call(xs, mod[l], norm_mlp_g[l], *mlp_w_bf16[l], seq,
                                  cast_jobs=jobs_for_next(l))
            keep_casts(l, casts)
        else:
            i = l - n_conv
            q_g_cols = jnp.broadcast_to((q_norm_g[i] * q_scale)[:, None], (HEAD_DIM, Q_TOKEN_TILE))
            bound = (np.sqrt(HEAD_DIM) * BOUND_SLACK) * jnp.max(jnp.abs(q_norm_g[i])) \
                * jnp.max(jnp.abs(k_norm_g))
            preshift_ok = bound <= MAX_PRESHIFT
            shift = jnp.where(preshift_ok, bound * LOG2E, 0.0).reshape(1).astype(F32)
            qT = _q_call(shift, xs, mod[l], norm_mix_g[l], q_wT_all, i, q_g_cols, fT,
                         sel_q, batch, seq)
            attn = lax.cond(
                preshift_ok,
                functools.partial(_attn_call, batch=batch, seq=seq, preshifted=True),
                functools.partial(_attn_call, batch=batch, seq=seq, preshifted=False),
                qT, k, vT)
            xs, casts = _mlp_call(xs, mod[l], norm_mlp_g[l], *mlp_w_bf16[l], seq,
                                  attn=attn, o_w=o_w_all, o_layer=i, cast_jobs=jobs_for_next(l))
            keep_casts(l, casts)
        if l == n_conv - 1:
            w_f3 = jnp.zeros((d, SLAB), F32).at[:, :3 * N_HEADS].set(jnp.tile(w_f, (1, 3)))
            b_f3 = jnp.zeros((1, SLAB), F32).at[0, :3 * N_HEADS].set(jnp.tile(b_f, 3))
            k, vT, fT = _kv_call(xs, kv_norm_g, w_kv[:, :d].astype(BF16),
                                 w_kv[:, d:].T.astype(BF16), w_f3.astype(BF16), b_f3,
                                 jnp.tile(k_norm_g, 2).reshape(1, SLAB), sel_k, batch, seq)
    return xs.reshape(batch, seq, d)
```
